```python
import math
import jax, jax.numpy as jnp
from jax import lax
import numpy as np

D_MODEL = 1024
BATCH = 2
SEQ = 8192
DEPTH = 2

HEAD_DIM = 64
MEM_LEN = 256
MEM_HEADS = 4
MEM_WIDTH = MEM_HEADS * HEAD_DIM
MIX_WIDTH = D_MODEL
CONV_CH = MIX_WIDTH - MEM_WIDTH
CONV_WIDTH = 31
SWA_Q_HEADS = 12
SWA_KV_HEADS = 4
SWA_GROUP = SWA_Q_HEADS // SWA_KV_HEADS
SWA_Q_WIDTH = SWA_Q_HEADS * HEAD_DIM
SWA_KV_WIDTH = SWA_KV_HEADS * HEAD_DIM
WINDOW = 128
BLOCK = 128
D_FF_DENSE = 2816
N_EXPERTS = 8
TOP_K = 2
D_FF_EXPERT = 3584
N_EVEN = (DEPTH + 1) // 2
N_ODD = DEPTH // 2
RMS_EPS = 1e-6
LN_EPS = 1e-5
ALIBI_MAX_BIAS = 8.0
NEG_INF = -1e30

kernel_name = "hybrid_conformer_swa_alibi_mem_moe"


def rms_norm(x, g):
    xf = x.astype(jnp.float32)
    y = xf * lax.rsqrt(jnp.mean(xf * xf, axis=-1, keepdims=True) + RMS_EPS)
    return (y * g.astype(jnp.float32)).astype(x.dtype)


def layer_norm(x, g, b):
    xf = x.astype(jnp.float32)
    mu = jnp.mean(xf, axis=-1, keepdims=True)
    var = jnp.mean(jnp.square(xf - mu), axis=-1, keepdims=True)
    y = (xf - mu) * lax.rsqrt(var + LN_EPS)
    return (y * g.astype(jnp.float32) + b.astype(jnp.float32)).astype(x.dtype)


def swiglu(h, w_gate, w_up, w_down):
    return (jax.nn.silu(h @ w_gate) * (h @ w_up)) @ w_down


def alibi_slopes(n_heads):
    return jnp.exp2(-ALIBI_MAX_BIAS * (jnp.arange(n_heads, dtype=jnp.float32) + 1.0) / n_heads)


def memory_attention(q, q_g, mem_k, mem_v):
    b, s, _ = q.shape
    q = rms_norm(q.reshape(b, s, MEM_HEADS, HEAD_DIM), q_g)
    sc = jnp.einsum('bshd,bmhd->bhsm', q, mem_k).astype(jnp.float32) * (HEAD_DIM ** -0.5)
    p = jax.nn.softmax(sc, axis=-1).astype(mem_v.dtype)
    o = jnp.einsum('bhsm,bmhd->bshd', p, mem_v)
    return o.reshape(b, s, MEM_WIDTH)


def conv_mixer(h, w_in, b_glu, dw_w, dw_b, ln_g, ln_b, memq_g, w_out, mem_k, mem_v):
    proj = h @ w_in
    a = proj[..., :CONV_CH] + b_glu[:CONV_CH]
    gate = proj[..., CONV_CH:2 * CONV_CH] + b_glu[CONV_CH:]
    q_mem = proj[..., 2 * CONV_CH:]
    u = a * jax.nn.sigmoid(gate)
    u_pad = jnp.pad(u, ((0, 0), (CONV_WIDTH - 1, 0), (0, 0)))
    c = lax.conv_general_dilated(
        u_pad, dw_w[:, None, :], window_strides=(1,), padding='VALID',
        dimension_numbers=('NWC', 'WIO', 'NWC'), feature_group_count=CONV_CH) + dw_b
    c = jax.nn.silu(layer_norm(c, ln_g, ln_b))
    m = memory_attention(q_mem, memq_g, mem_k, mem_v)
    return jnp.concatenate([c, m], axis=-1) @ w_out


def swa_mixer(h, w_in, q_g, k_g, sinks, memq_g, w_out, mem_k, mem_v):
    b, s, _ = h.shape
    nb = s // BLOCK
    proj = h @ w_in
    o1 = SWA_Q_WIDTH
    o2 = o1 + SWA_KV_WIDTH
    o3 = o2 + SWA_KV_WIDTH
    q = rms_norm(proj[..., :o1].reshape(b, s, SWA_KV_HEADS, SWA_GROUP, HEAD_DIM), q_g)
    k = rms_norm(proj[..., o1:o2].reshape(b, s, SWA_KV_HEADS, HEAD_DIM), k_g)
    v = proj[..., o2:o3].reshape(b, s, SWA_KV_HEADS, HEAD_DIM)
    q_mem = proj[..., o3:]

    qb = q.reshape(b, nb, BLOCK, SWA_KV_HEADS, SWA_GROUP, HEAD_DIM)
    def band(t):
        prev = jnp.pad(t, ((0, 0), (BLOCK, 0), (0, 0), (0, 0)))[:, :s]
        return jnp.concatenate([prev.reshape(b, nb, BLOCK, SWA_KV_HEADS, HEAD_DIM),
                                t.reshape(b, nb, BLOCK, SWA_KV_HEADS, HEAD_DIM)], axis=2)
    kk = band(k)
    vv = band(v)
    sc = jnp.einsum('bnqhgd,bnchd->bnhgqc', qb, kk).astype(jnp.float32) * (HEAD_DIM ** -0.5)

    qpos = jnp.arange(BLOCK) + BLOCK
    kpos = jnp.arange(2 * BLOCK)
    dist = (qpos[:, None] - kpos[None, :]).astype(jnp.float32)
    win = (dist >= 0) & (dist < WINDOW)
    first = (jnp.arange(nb)[:, None] > 0) | (kpos[None, :] >= BLOCK)
    mask = win[None, :, :] & first[:, None, :]
    slopes = alibi_slopes(SWA_Q_HEADS).reshape(SWA_KV_HEADS, SWA_GROUP)
    bias = -slopes[:, :, None, None] * dist[None, None]
    sc = jnp.where(mask[None, :, None, None], sc + bias[None, None], NEG_INF)

    sink = jnp.broadcast_to(sinks.astype(jnp.float32).reshape(1, 1, SWA_KV_HEADS, SWA_GROUP, 1, 1),
                            sc.shape[:-1] + (1,))
    p = jax.nn.softmax(jnp.concatenate([sc, sink], axis=-1), axis=-1)[..., :-1]
    o = jnp.einsum('bnhgqc,bnchd->bnqhgd', p.astype(vv.dtype), vv).reshape(b, s, SWA_Q_WIDTH)
    m = memory_attention(q_mem, memq_g, mem_k, mem_v)
    return jnp.concatenate([o, m], axis=-1) @ w_out


def moe_swiglu(h, router, we_gate, we_up, we_down):
    logits = (h @ router).astype(jnp.float32)
    top_v, top_i = lax.top_k(logits, TOP_K)
    top_w = jax.nn.softmax(top_v, axis=-1)
    gates = jnp.sum(jax.nn.one_hot(top_i, N_EXPERTS, dtype=jnp.float32) * top_w[..., None], axis=-2)
    gates = gates.astype(h.dtype)
    out = jnp.zeros_like(h)
    for e in range(N_EXPERTS):
        out = out + gates[..., e:e + 1] * swiglu(h, we_gate[e], we_up[e], we_down[e])
    return out


def setup_inputs(seed: int = 0) -> dict:
    key = jax.random.key(seed)
    ks = iter(jax.random.split(key, 40))
    f32 = jnp.float32
    def nrm(shape, scale):
        return jax.random.normal(next(ks), shape, f32) * scale
    def gain(shape):
        return 1.0 + 0.05 * jax.random.normal(next(ks), shape, f32)
    D = D_MODEL
    return {
        "x": nrm((BATCH, SEQ, D), 1.0),
        "mem": nrm((BATCH, MEM_LEN, D), 1.0),
        "mem_norm_g": gain((D,)),
        "w_mem_kv": nrm((D, 2 * MEM_WIDTH), D ** -0.5),
        "mem_k_norm_g": gain((HEAD_DIM,)),
        "cv_attn_norm_g": gain((N_EVEN, D)),
        "cv_w_in": nrm((N_EVEN, D, 2 * CONV_CH + MEM_WIDTH), D ** -0.5),
        "cv_b_glu": nrm((N_EVEN, 2 * CONV_CH), 0.02),
        "cv_dw_w": nrm((N_EVEN, CONV_WIDTH, CONV_CH), CONV_WIDTH ** -0.5),
        "cv_dw_b": nrm((N_EVEN, CONV_CH), 0.02),
        "cv_ln_g": gain((N_EVEN, CONV_CH)),
        "cv_ln_b": nrm((N_EVEN, CONV_CH), 0.02),
        "cv_memq_norm_g": gain((N_EVEN, HEAD_DIM)),
        "cv_w_out": nrm((N_EVEN, MIX_WIDTH, D), MIX_WIDTH ** -0.5),
        "cv_ffn_norm_g": gain((N_EVEN, D)),
        "cv_w_gate": nrm((N_EVEN, D, D_FF_DENSE), D ** -0.5),
        "cv_w_up": nrm((N_EVEN, D, D_FF_DENSE), D ** -0.5),
        "cv_w_down": nrm((N_EVEN, D_FF_DENSE, D), D_FF_DENSE ** -0.5),
        "sw_attn_norm_g": gain((N_ODD, D)),
        "sw_w_in": nrm((N_ODD, D, SWA_Q_WIDTH + 2 * SWA_KV_WIDTH + MEM_WIDTH), D ** -0.5),
        "sw_q_norm_g": gain((N_ODD, HEAD_DIM)),
        "sw_k_norm_g": gain((N_ODD, HEAD_DIM)),
        "sw_sinks": nrm((N_ODD, SWA_Q_HEADS), 0.5),
        "sw_memq_norm_g": gain((N_ODD, HEAD_DIM)),
        "sw_w_out": nrm((N_ODD, MIX_WIDTH, D), MIX_WIDTH ** -0.5),
        "sw_ffn_norm_g": gain((N_ODD, D)),
        "sw_router": nrm((N_ODD, D, N_EXPERTS), D ** -0.5),
        "sw_we_gate": nrm((N_ODD, N_EXPERTS, D, D_FF_EXPERT), D ** -0.5),
        "sw_we_up": nrm((N_ODD, N_EXPERTS, D, D_FF_EXPERT), D ** -0.5),
        "sw_we_down": nrm((N_ODD, N_EXPERTS, D_FF_EXPERT, D), D_FF_EXPERT ** -0.5),
    }


def reference(x, mem, mem_norm_g, w_mem_kv, mem_k_norm_g,
              cv_attn_norm_g, cv_w_in, cv_b_glu, cv_dw_w, cv_dw_b, cv_ln_g, cv_ln_b,
              cv_memq_norm_g, cv_w_out, cv_ffn_norm_g, cv_w_gate, cv_w_up, cv_w_down,
              sw_attn_norm_g, sw_w_in, sw_q_norm_g, sw_k_norm_g, sw_sinks, sw_memq_norm_g,
              sw_w_out, sw_ffn_norm_g, sw_router, sw_we_gate, sw_we_up, sw_we_down):
    b, m_len, _ = mem.shape
    mkv = rms_norm(mem, mem_norm_g) @ w_mem_kv
    mem_k = rms_norm(mkv[..., :MEM_WIDTH].reshape(b, m_len, MEM_HEADS, HEAD_DIM), mem_k_norm_g)
    mem_v = mkv[..., MEM_WIDTH:].reshape(b, m_len, MEM_HEADS, HEAD_DIM)

    h = x
    for i in range(DEPTH):
        j = i // 2
        if i % 2 == 0:
            hn = rms_norm(h, cv_attn_norm_g[j])
            h = h + conv_mixer(hn, cv_w_in[j], cv_b_glu[j], cv_dw_w[j], cv_dw_b[j], cv_ln_g[j],
                               cv_ln_b[j], cv_memq_norm_g[j], cv_w_out[j], mem_k, mem_v)
            hn = rms_norm(h, cv_ffn_norm_g[j])
            h = h + swiglu(hn, cv_w_gate[j], cv_w_up[j], cv_w_down[j])
        else:
            hn = rms_norm(h, sw_attn_norm_g[j])
            h = h + swa_mixer(hn, sw_w_in[j], sw_q_norm_g[j], sw_k_norm_g[j], sw_sinks[j],
                              sw_memq_norm_g[j], sw_w_out[j], mem_k, mem_v)
            hn = rms_norm(h, sw_ffn_norm_g[j])
            h = h + moe_swiglu(hn, sw_router[j], sw_we_gate[j], sw_we_up[j], sw_we_down[j])
    return h
```

```python
import functools

import jax
import jax.numpy as jnp
from jax import lax
from jax.experimental import pallas as pl
from jax.experimental.pallas import tpu as pltpu

F32 = jnp.float32
BF16 = jnp.bfloat16

HEAD_DIM = 64
MEM_HEADS = 4
MEM_WIDTH = MEM_HEADS * HEAD_DIM
CONV_WIDTH = 31
CONV_HALO = 32
SWA_KV_HEADS = 4
SWA_GROUP = 3
SWA_BLOCK = 128
N_EXPERTS = 8
RMS_EPS = 1e-6
LN_EPS = 1e-5
ALIBI_MAX_BIAS = 8.0
NEG_INF = -1e30
LANES = 128
VMEM_LIMIT = 56 * 1024 * 1024

TM_MIX = 512
TM_FFN = 512
FFN_SPLIT = 1536
TM_ROUTE = 512
TS_MOE = 512
TF_MOE = 1792


def _dot(a, b):
    return jnp.dot(a, b, preferred_element_type=F32)


def _dot_nt(a, b):
    return lax.dot_general(a, b, (((1,), (1,)), ((), ())), preferred_element_type=F32)


def _rms_rows(x, g):
    ms = jnp.mean(x * x, axis=-1, keepdims=True)
    return x * lax.rsqrt(ms + RMS_EPS) * g


def _head_mean_sq(x, head_ones):
    sq = x * x
    hi = sq.astype(BF16)
    lo = (sq - hi.astype(F32)).astype(BF16)
    return (_dot(hi, head_ones) + _dot(lo, head_ones)) * (1.0 / HEAD_DIM)


def _head_rms(x, head_ones, g):
    return x * lax.rsqrt(_head_mean_sq(x, head_ones) + RMS_EPS) * g


def _head_lane_mask(h):
    lane = lax.broadcasted_iota(jnp.int32, (1, MEM_WIDTH), 1)
    return ((lane >= h * HEAD_DIM) & (lane < (h + 1) * HEAD_DIM)).astype(F32)


def _mem_attention(q_mem, head_ones, qg_scaled, kexp, vexp):
    qn = _head_rms(q_mem, head_ones, qg_scaled).astype(BF16)
    s = _dot_nt(qn, kexp)
    m_len = kexp.shape[0] // MEM_HEADS
    ps = []
    for h in range(MEM_HEADS):
        sh = s[:, h * m_len:(h + 1) * m_len]
        e = jnp.exp(sh - jnp.max(sh, axis=-1, keepdims=True))
        ps.append((e / jnp.sum(e, axis=-1, keepdims=True)).astype(BF16))
    return _dot(jnp.concatenate(ps, axis=-1), vexp)


def _mem_kv_kernel(mem_ref, g_ref, w_ref, kg_ref, ones_ref, kexp_ref, vexp_ref):
    x = mem_ref[0]
    hn = _rms_rows(x, g_ref[...]).astype(BF16)
    kv = _dot(hn, w_ref[...])
    k = _head_rms(kv[:, :MEM_WIDTH], ones_ref[...], kg_ref[...])
    v = kv[:, MEM_WIDTH:]
    m_len = x.shape[0]
    for h in range(MEM_HEADS):
        mask = _head_lane_mask(h)
        kexp_ref[0, h * m_len:(h + 1) * m_len, :] = (k * mask).astype(BF16)
        vexp_ref[0, h * m_len:(h + 1) * m_len, :] = (v * mask).astype(BF16)


def _mem_kv(mem, g, w_bf, kg_tiled, head_ones):
    b, m_len, d = mem.shape
    out = jax.ShapeDtypeStruct((b, MEM_HEADS * m_len, MEM_WIDTH), BF16)
    const = lambda shape: pl.BlockSpec(shape, lambda i: (0,) * len(shape))
    return pl.pallas_call(
        _mem_kv_kernel,
        out_shape=(out, out),
        grid=(b,),
        in_specs=[pl.BlockSpec((1, m_len, d), lambda i: (i, 0, 0)),
                  const((1, d)), const(w_bf.shape), const((1, MEM_WIDTH)), const(head_ones.shape)],
        out_specs=(pl.BlockSpec((1, MEM_HEADS * m_len, MEM_WIDTH), lambda i: (i, 0, 0)),) * 2,
        name="mem_kv",
    )(mem, g.reshape(1, d), w_bf, kg_tiled, head_ones)


def _conv_mixer_kernel(x_ref, g_ref, win_ref, bglu_ref, dww_ref, dwb_ref, lng_ref, lnb_ref,
                       mqg_ref, ones_ref, kexp_ref, vexp_ref, wout_ref, o_ref, ubuf, cbuf):
    tm = x_ref.shape[1]
    cc = dww_ref.shape[1]

    @pl.when(pl.program_id(1) == 0)
    def _():
        ubuf[0:CONV_HALO, :] = jnp.zeros((CONV_HALO, cc), F32)

    x = x_ref[0]
    hn = _rms_rows(x, g_ref[...]).astype(BF16)
    proj = _dot(hn, win_ref[...])
    a = proj[:, :cc] + bglu_ref[:, :cc]
    gate = proj[:, cc:2 * cc] + bglu_ref[:, cc:]
    ubuf[CONV_HALO:CONV_HALO + tm, :] = a * jax.nn.sigmoid(gate)

    rows = 32
    base = CONV_HALO - (CONV_WIDTH - 1)
    for r0 in range(0, tm, rows):
        acc = jnp.broadcast_to(dwb_ref[...], (rows, cc))
        for k in range(CONV_WIDTH):
            acc = acc + ubuf[r0 + base + k:r0 + base + k + rows, :] * dww_ref[k:k + 1, :]
        mu = jnp.mean(acc, axis=-1, keepdims=True)
        dlt = acc - mu
        var = jnp.mean(dlt * dlt, axis=-1, keepdims=True)
        y = dlt * lax.rsqrt(var + LN_EPS) * lng_ref[...] + lnb_ref[...]
        cbuf[r0:r0 + rows, :] = (y * jax.nn.sigmoid(y)).astype(BF16)

    ubuf[0:CONV_HALO, :] = ubuf[tm:tm + CONV_HALO, :]

    m = _mem_attention(proj[:, 2 * cc:], ones_ref[...], mqg_ref[...], kexp_ref[0], vexp_ref[0])
    out = _dot(cbuf[...], wout_ref[0:cc, :]) + _dot(m.astype(BF16), wout_ref[cc:, :])
    o_ref[0] = x + out


def _conv_mixer(h, g, win_bf, bglu, dww, dwb, lng, lnb, mqg_scaled, head_ones, kexp, vexp, wout_bf):
    b, s, d = h.shape
    cc = dww.shape[1]
    tm = TM_MIX
    const = lambda shape: pl.BlockSpec(shape, lambda i, j: (0,) * len(shape))
    per_batch = lambda arr: pl.BlockSpec((1,) + arr.shape[1:], lambda i, j: (i, 0, 0))
    return pl.pallas_call(
        _conv_mixer_kernel,
        out_shape=jax.ShapeDtypeStruct(h.shape, F32),
        grid=(b, s // tm),
        in_specs=[pl.BlockSpec((1, tm, d), lambda i, j: (i, j, 0)),
                  const((1, d)), const(win_bf.shape), const((1, 2 * cc)), const(dww.shape),
                  const((1, cc)), const((1, cc)), const((1, cc)), const((1, MEM_WIDTH)),
                  const(head_ones.shape), per_batch(kexp), per_batch(vexp), const(wout_bf.shape)],
        out_specs=pl.BlockSpec((1, tm, d), lambda i, j: (i, j, 0)),
        scratch_shapes=[pltpu.VMEM((CONV_HALO + tm, cc), F32), pltpu.VMEM((tm, cc), BF16)],
        compiler_params=pltpu.CompilerParams(dimension_semantics=("arbitrary", "arbitrary"),
                                             vmem_limit_bytes=VMEM_LIMIT),
        name="conv_mixer",
    )(h, g.reshape(1, d), win_bf, bglu.reshape(1, -1), dww, dwb.reshape(1, cc), lng.reshape(1, cc),
      lnb.reshape(1, cc), mqg_scaled, head_ones, kexp, vexp, wout_bf)


def _dense_ffn_kernel(x_ref, g_ref, wg_ref, wu_ref, wd_ref, o_ref):
    x = x_ref[...]
    hn = _rms_rows(x, g_ref[...]).astype(BF16)
    f = wg_ref.shape[1]
    out = x
    for lo, hi in ((0, FFN_SPLIT), (FFN_SPLIT, f)):
        gte = _dot(hn, wg_ref[:, lo:hi])
        up = _dot(hn, wu_ref[:, lo:hi])
        act = (gte * jax.nn.sigmoid(gte) * up).astype(BF16)
        out = out + _dot(act, wd_ref[lo:hi, :])
    o_ref[...] = out


def _dense_ffn(h2d, g, wg_bf, wu_bf, wd_bf):
    t, d = h2d.shape
    tm = TM_FFN
    resident = lambda shape: pl.BlockSpec(shape, lambda i: (0, 0), pipeline_mode=pl.Buffered(1))
    return pl.pallas_call(
        _dense_ffn_kernel,
        out_shape=jax.ShapeDtypeStruct((t, d), F32),
        grid=(t // tm,),
        in_specs=[pl.BlockSpec((tm, d), lambda i: (i, 0)), resident((1, d)),
                  resident(wg_bf.shape), resident(wu_bf.shape), resident(wd_bf.shape)],
        out_specs=pl.BlockSpec((tm, d), lambda i: (i, 0)),
        compiler_params=pltpu.CompilerParams(dimension_semantics=("arbitrary",),
                                             vmem_limit_bytes=VMEM_LIMIT),
        name="dense_ffn",
    )(h2d, g.reshape(1, d), wg_bf, wu_bf, wd_bf)


def _alibi_slope(head):
    n_heads = SWA_KV_HEADS * SWA_GROUP
    return 2.0 ** (-ALIBI_MAX_BIAS * (head + 1.0) / n_heads)


def _swa_mixer_kernel(sinks_ref, x_ref, g_ref, win_ref, qg_ref, kg_ref, mqg_ref, ones_ref,
                      kexp_ref, vexp_ref, wout_ref, o_ref, kbuf, vbuf, obuf):
    tm = x_ref.shape[1]
    blk = SWA_BLOCK
    kvw = SWA_KV_HEADS * HEAD_DIM
    qw = SWA_GROUP * kvw
    first_tile = pl.program_id(1) == 0

    @pl.when(first_tile)
    def _():
        kbuf[:, 0:blk, :] = jnp.zeros((SWA_KV_HEADS, blk, kvw), BF16)
        vbuf[:, 0:blk, :] = jnp.zeros((SWA_KV_HEADS, blk, kvw), BF16)

    x = x_ref[0]
    hn = _rms_rows(x, g_ref[...]).astype(BF16)
    proj = _dot(hn, win_ref[...])
    ones = ones_ref[...]
    kn = _head_rms(proj[:, qw:qw + kvw], ones, kg_ref[...])
    v = proj[:, qw + kvw:qw + 2 * kvw]
    for h in range(SWA_KV_HEADS):
        mask = _head_lane_mask(h)
        kbuf[h, blk:blk + tm, :] = (kn * mask).astype(BF16)
        vbuf[h, blk:blk + tm, :] = (v * mask).astype(BF16)

    qi = lax.broadcasted_iota(jnp.int32, (blk, 2 * blk), 0)
    ci = lax.broadcasted_iota(jnp.int32, (blk, 2 * blk), 1)
    dist_i = qi + blk - ci
    dist = dist_i.astype(F32)
    win = (dist_i >= 0) & (dist_i < blk)
    win_first = win & ((ci >= blk) | jnp.logical_not(first_tile))

    for g in range(SWA_GROUP):
        qn = _head_rms(proj[:, g * kvw:(g + 1) * kvw], ones, qg_ref[...]).astype(BF16)
        for n in range(tm // blk):
            mask = win_first if n == 0 else win
            qb = qn[n * blk:(n + 1) * blk, :]
            og = jnp.zeros((blk, kvw), F32)
            for h in range(SWA_KV_HEADS):
                head = h * SWA_GROUP + g
                s = _dot_nt(qb, kbuf[h, n * blk:(n + 2) * blk, :])
                s = jnp.where(mask, s - _alibi_slope(head) * dist, NEG_INF)
                sink = sinks_ref[head]
                mx = jnp.maximum(jnp.max(s, axis=-1, keepdims=True), sink)
                e = jnp.exp(s - mx)
                den = jnp.sum(e, axis=-1, keepdims=True) + jnp.exp(sink - mx)
                og = og + _dot((e / den).astype(BF16), vbuf[h, n * blk:(n + 2) * blk, :])
            obuf[n * blk:(n + 1) * blk, g * kvw:(g + 1) * kvw] = og.astype(BF16)

    kbuf[:, 0:blk, :] = kbuf[:, tm:tm + blk, :]
    vbuf[:, 0:blk, :] = vbuf[:, tm:tm + blk, :]

    m = _mem_attention(proj[:, qw + 2 * kvw:], ones, mqg_ref[...], kexp_ref[0], vexp_ref[0])
    out = _dot(obuf[...], wout_ref[0:qw, :]) + _dot(m.astype(BF16), wout_ref[qw:, :])
    o_ref[0] = x + out


def _swa_mixer(h, g, win_bf, qg_scaled, kg_tiled, sinks, mqg_scaled, head_ones, kexp, vexp, wout_bf):
    b, s, d = h.shape
    tm = TM_MIX
    kvw = SWA_KV_HEADS * HEAD_DIM
    const = lambda shape: pl.BlockSpec(shape, lambda i, j, sk: (0,) * len(shape))
    per_batch = lambda arr: pl.BlockSpec((1,) + arr.shape[1:], lambda i, j, sk: (i, 0, 0))
    grid_spec = pltpu.PrefetchScalarGridSpec(
        num_scalar_prefetch=1,
        grid=(b, s // tm),
        in_specs=[pl.BlockSpec((1, tm, d), lambda i, j, sk: (i, j, 0)),
                  const((1, d)), const(win_bf.shape), const((1, kvw)), const((1, kvw)),
                  const((1, MEM_WIDTH)), const(head_ones.shape), per_batch(kexp), per_batch(vexp),
                  const(wout_bf.shape)],
        out_specs=pl.BlockSpec((1, tm, d), lambda i, j, sk: (i, j, 0)),
        scratch_shapes=[pltpu.VMEM((SWA_KV_HEADS, SWA_BLOCK + tm, kvw), BF16),
                        pltpu.VMEM((SWA_KV_HEADS, SWA_BLOCK + tm, kvw), BF16),
                        pltpu.VMEM((tm, SWA_GROUP * kvw), BF16)])
    return pl.pallas_call(
        _swa_mixer_kernel,
        out_shape=jax.ShapeDtypeStruct(h.shape, F32),
        grid_spec=grid_spec,
        compiler_params=pltpu.CompilerParams(dimension_semantics=("arbitrary", "arbitrary"),
                                             vmem_limit_bytes=VMEM_LIMIT),
        name="swa_mixer",
    )(sinks, h, g.reshape(1, d), win_bf, qg_scaled, kg_tiled, mqg_scaled, head_ones, kexp, vexp, wout_bf)


def _router_kernel(x_ref, g_ref, r_ref, hn_ref, info_ref, cnt_ref, run):
    tm = x_ref.shape[0]

    @pl.when(pl.program_id(0) == 0)
    def _():
        run[...] = jnp.zeros_like(run)

    hn = _rms_rows(x_ref[...], g_ref[...])
    hn_ref[...] = hn
    logits = jnp.dot(hn, r_ref[...], precision=lax.Precision.HIGHEST, preferred_element_type=F32)
    lane = lax.broadcasted_iota(jnp.int32, (tm, LANES), 1).astype(F32)
    logits = jnp.where(lane < N_EXPERTS, logits, -jnp.inf)
    m1 = jnp.max(logits, axis=-1, keepdims=True)
    e1 = jnp.min(jnp.where(logits == m1, lane, float(LANES)), axis=-1, keepdims=True)
    oh1 = lane == e1
    rest = jnp.where(oh1, -jnp.inf, logits)
    m2 = jnp.max(rest, axis=-1, keepdims=True)
    e2 = jnp.min(jnp.where(rest == m2, lane, float(LANES)), axis=-1, keepdims=True)
    oh2 = lane == e2
    z = jnp.exp(m2 - m1)
    w1 = 1.0 / (1.0 + z)
    w2 = z / (1.0 + z)

    both = jnp.where(oh1 | oh2, 1.0, 0.0)
    ri = lax.broadcasted_iota(jnp.int32, (tm, tm), 0)
    ci = lax.broadcasted_iota(jnp.int32, (tm, tm), 1)
    before = jnp.where(ci < ri, 1.0, 0.0).astype(BF16)
    cnt = _dot(before, both.astype(BF16)) + run[...]
    r1 = jnp.sum(jnp.where(oh1, cnt, 0.0), axis=-1, keepdims=True)
    r2 = jnp.sum(jnp.where(oh2, cnt, 0.0), axis=-1, keepdims=True)
    run[...] = run[...] + jnp.sum(both, axis=0, keepdims=True)
    cnt_ref[...] = run[...]

    info = jnp.zeros((tm, LANES), F32)
    for idx, val in enumerate((e1, e2, r1, r2, w1, w2)):
        info = jnp.where(lane == idx, val, info)
    info_ref[...] = info


def _router(h2d, g, router_pad):
    t, d = h2d.shape
    tm = TM_ROUTE
    const = lambda shape: pl.BlockSpec(shape, lambda i: (0, 0))
    return pl.pallas_call(
        _router_kernel,
        out_shape=(jax.ShapeDtypeStruct((t, d), F32), jax.ShapeDtypeStruct((t, LANES), F32),
                   jax.ShapeDtypeStruct((1, LANES), F32)),
        grid=(t // tm,),
        in_specs=[pl.BlockSpec((tm, d), lambda i: (i, 0)), const((1, d)), const(router_pad.shape)],
        out_specs=(pl.BlockSpec((tm, d), lambda i: (i, 0)), pl.BlockSpec((tm, LANES), lambda i: (i, 0)),
                   const((1, LANES))),
        scratch_shapes=[pltpu.VMEM((1, LANES), F32)],
        compiler_params=pltpu.CompilerParams(dimension_semantics=("arbitrary",)),
        name="router",
    )(h2d, g.reshape(1, d), router_pad)


def _row_copy(src, src_row, dst, dst_row, sem):
    return pltpu.make_async_copy(src.at[pl.ds(src_row, 1)], dst.at[pl.ds(dst_row, 1)], sem)


def _moe_ffn_kernel(te_ref, code_ref, nvt_ref, hn_hbm, wg_ref, wu_ref, wd_ref, y_hbm,
                    xs, acc, gsem, ssem):
    ts = xs.shape[0]
    i = pl.program_id(0)
    j = pl.program_id(1)
    valid = i < nvt_ref[0]

    @pl.when(valid & (j == 0))
    def _():
        def issue(r, c):
            code = code_ref[i * ts + r]
            tok = jnp.where(code < 0, 0, lax.shift_right_logical(code, 1))
            _row_copy(hn_hbm, tok, xs, r, gsem).start()
            return c
        lax.fori_loop(0, ts, issue, 0)

        def wait(r, c):
            _row_copy(hn_hbm, 0, xs, r, gsem).wait()
            return c
        lax.fori_loop(0, ts, wait, 0)
        acc[...] = jnp.zeros_like(acc)

    @pl.when(valid)
    def _():
        x = xs[...].astype(BF16)
        gte = _dot(x, wg_ref[0])
        up = _dot(x, wu_ref[0])
        act = (gte * jax.nn.sigmoid(gte) * up).astype(BF16)
        acc[...] += _dot(act, wd_ref[0])

    @pl.when(valid & (j == pl.num_programs(1) - 1))
    def _():
        def issue(r, c):
            code = code_ref[i * ts + r]

            @pl.when(code >= 0)
            def _():
                _row_copy(acc, r, y_hbm, code, ssem).start()
            return c
        lax.fori_loop(0, ts, issue, 0)

        def wait(r, c):
            @pl.when(code_ref[i * ts + r] >= 0)
            def _():
                _row_copy(acc, r, y_hbm, 0, ssem).wait()
            return c
        lax.fori_loop(0, ts, wait, 0)


def _moe_ffn(tile_expert, slot_code, n_valid_tiles, hn2, wg_bf, wu_bf, wd_bf):
    t, d = hn2.shape
    n_exp, _, f = wg_bf.shape
    ts, tf = TS_MOE, TF_MOE
    n_tiles = slot_code.shape[0] // ts
    grid_spec = pltpu.PrefetchScalarGridSpec(
        num_scalar_prefetch=3,
        grid=(n_tiles, f // tf),
        in_specs=[pl.BlockSpec(memory_space=pl.ANY),
                  pl.BlockSpec((1, d, tf), lambda i, j, te, code, nvt: (te[i], 0, j)),
                  pl.BlockSpec((1, d, tf), lambda i, j, te, code, nvt: (te[i], 0, j)),
                  pl.BlockSpec((1, tf, d), lambda i, j, te, code, nvt: (te[i], j, 0))],
        out_specs=pl.BlockSpec(memory_space=pl.ANY),
        scratch_shapes=[pltpu.VMEM((ts, d), F32), pltpu.VMEM((ts, d), F32),
                        pltpu.SemaphoreType.DMA, pltpu.SemaphoreType.DMA])
    return pl.pallas_call(
        _moe_ffn_kernel,
        out_shape=jax.ShapeDtypeStruct((2 * t, d), F32),
        grid_spec=grid_spec,
        compiler_params=pltpu.CompilerParams(dimension_semantics=("arbitrary", "arbitrary"),
                                             vmem_limit_bytes=VMEM_LIMIT),
        name="moe_ffn",
    )(tile_expert, slot_code, n_valid_tiles, hn2, wg_bf, wu_bf, wd_bf)


def _combine_kernel(h_ref, y_ref, info_ref, o_ref):
    d = h_ref.shape[1]
    w1 = info_ref[:, 4:5]
    w2 = info_ref[:, 5:6]
    o_ref[...] = h_ref[...] + (w1 * y_ref[:, :d] + w2 * y_ref[:, d:])


def _combine(h2d, y_pairs, info):
    t, d = h2d.shape
    tm = TM_FFN
    return pl.pallas_call(
        _combine_kernel,
        out_shape=jax.ShapeDtypeStruct((t, d), F32),
        grid=(t // tm,),
        in_specs=[pl.BlockSpec((tm, d), lambda i: (i, 0)), pl.BlockSpec((tm, 2 * d), lambda i: (i, 0)),
                  pl.BlockSpec((tm, LANES), lambda i: (i, 0))],
        out_specs=pl.BlockSpec((tm, d), lambda i: (i, 0)),
        compiler_params=pltpu.CompilerParams(dimension_semantics=("arbitrary",)),
        name="moe_combine",
    )(h2d, y_pairs, info)


def _moe(h2d, g, router, wg_bf, wu_bf, wd_bf):
    t, d = h2d.shape
    ts = TS_MOE
    router_pad = jnp.pad(router, ((0, 0), (0, LANES - N_EXPERTS)))
    hn2, info, counts = _router(h2d, g, router_pad)

    counts = counts[0, :N_EXPERTS].astype(jnp.int32)
    padded = (counts + ts - 1) // ts * ts
    ends = jnp.cumsum(padded)
    starts = ends - padded
    experts = info[:, 0:2].astype(jnp.int32)
    ranks = info[:, 2:4].astype(jnp.int32)
    slots = starts[experts] + ranks
    n_slots = 2 * t + N_EXPERTS * ts
    codes = 2 * jnp.arange(t, dtype=jnp.int32)[:, None] + jnp.arange(2, dtype=jnp.int32)[None, :]
    slot_code = jnp.full((n_slots,), -1, jnp.int32).at[slots.reshape(-1)].set(
        codes.reshape(-1), unique_indices=True, indices_are_sorted=False)
    tile_start = jnp.arange(n_slots // ts, dtype=jnp.int32) * ts
    tile_expert = jnp.minimum(jnp.searchsorted(ends, tile_start, side="right"), N_EXPERTS - 1).astype(jnp.int32)
    n_valid_tiles = (ends[-1] // ts).astype(jnp.int32).reshape(1)
    last_expert = tile_expert[jnp.maximum(n_valid_tiles[0] - 1, 0)]
    tile_expert = jnp.where(tile_start < ends[-1], tile_expert, last_expert)

    y = _moe_ffn(tile_expert, slot_code, n_valid_tiles, hn2, wg_bf, wu_bf, wd_bf)
    return _combine(h2d, y.reshape(-1, 2 * d), info)


def _regroup_heads(w, axis):
    shape = w.shape
    split = shape[:axis] + (SWA_KV_HEADS, SWA_GROUP, HEAD_DIM) + shape[axis + 1:]
    return jnp.swapaxes(w.reshape(split), axis, axis + 1).reshape(shape)


def kernel(x, mem, mem_norm_g, w_mem_kv, mem_k_norm_g, cv_attn_norm_g, cv_w_in, cv_b_glu, cv_dw_w, cv_dw_b, cv_ln_g, cv_ln_b, cv_memq_norm_g, cv_w_out, cv_ffn_norm_g, cv_w_gate, cv_w_up, cv_w_down, sw_attn_norm_g, sw_w_in, sw_q_norm_g, sw_k_norm_g, sw_sinks, sw_memq_norm_g, sw_w_out, sw_ffn_norm_g, sw_router, sw_we_gate, sw_we_up, sw_we_down):
    b, s, d = x.shape
    scale = HEAD_DIM ** -0.5
    lane_head = jnp.arange(MEM_WIDTH) // HEAD_DIM
    head_ones = (lane_head[:, None] == lane_head[None, :]).astype(BF16)
    tile4 = lambda gvec: jnp.tile(gvec, MEM_HEADS).reshape(1, MEM_WIDTH)

    kexp, vexp = _mem_kv(mem, mem_norm_g, w_mem_kv.astype(BF16), tile4(mem_k_norm_g), head_ones)

    h = x
    depth = cv_w_in.shape[0] + sw_w_in.shape[0]
    for i in range(depth):
        j = i // 2
        if i % 2 == 0:
            h = _conv_mixer(h, cv_attn_norm_g[j], cv_w_in[j].astype(BF16), cv_b_glu[j], cv_dw_w[j], cv_dw_b[j],
                            cv_ln_g[j], cv_ln_b[j], tile4(cv_memq_norm_g[j]) * scale, head_ones, kexp, vexp,
                            cv_w_out[j].astype(BF16))
            h = _dense_ffn(h.reshape(b * s, d), cv_ffn_norm_g[j], cv_w_gate[j].astype(BF16),
                           cv_w_up[j].astype(BF16), cv_w_down[j].astype(BF16)).reshape(b, s, d)
        else:
            qw = SWA_KV_HEADS * SWA_GROUP * HEAD_DIM
            w_in = jnp.concatenate([_regroup_heads(sw_w_in[j][:, :qw], 1), sw_w_in[j][:, qw:]], axis=1)
            w_out = jnp.concatenate([_regroup_heads(sw_w_out[j][:qw], 0), sw_w_out[j][qw:]], axis=0)
            h = _swa_mixer(h, sw_attn_norm_g[j], w_in.astype(BF16), tile4(sw_q_norm_g[j]) * scale,
                           tile4(sw_k_norm_g[j]), sw_sinks[j], tile4(sw_memq_norm_g[j]) * scale, head_ones,
                           kexp, vexp, w_out.astype(BF16))
            h = _moe(h.reshape(b * s, d), sw_ffn_norm_g[j], sw_router[j], sw_we_gate[j].astype(BF16),
                     sw_we_up[j].astype(BF16), sw_we_down[j].astype(BF16)).reshape(b, s, d)
    return h
```

```python
import jax
import jax.numpy as jnp
from jax import lax
from jax.experimental import pallas as pl
from jax.experimental.pallas import tpu as pltpu

F32 = jnp.float32
BF16 = jnp.bfloat16

HEAD_DIM = 64
MEM_HEADS = 4
MEM_WIDTH = MEM_HEADS * HEAD_DIM
CONV_WIDTH = 31
CONV_HALO = 32
SWA_KV_HEADS = 4
SWA_GROUP = 3
SWA_BLOCK = 128
N_EXPERTS = 8
RMS_EPS = 1e-6
LN_EPS = 1e-5
ALIBI_MAX_BIAS = 8.0
NEG_INF = -1e30
LANES = 128
SUBLANES = 8
VMEM_LIMIT = 56 * 1024 * 1024

TM_MIX = 512
TM_FFN = 512
FFN_SPLIT = 1536
TM_ROUTE = 512
TS_MOE = 512
TF_MOE = 1792


def _dot(a, b):
    return jnp.dot(a, b, preferred_element_type=F32)


def _dot_nt(a, b):
    return lax.dot_general(a, b, (((1,), (1,)), ((), ())), preferred_element_type=F32)


def _rms_rows(x, g):
    ms = jnp.mean(x * x, axis=-1, keepdims=True)
    return x * lax.rsqrt(ms + RMS_EPS) * g


def _head_mean_sq(x, head_ones):
    sq = x * x
    hi = sq.astype(BF16)
    lo = (sq - hi.astype(F32)).astype(BF16)
    return (_dot(hi, head_ones) + _dot(lo, head_ones)) * (1.0 / HEAD_DIM)


def _head_rms(x, head_ones, g):
    return x * lax.rsqrt(_head_mean_sq(x, head_ones) + RMS_EPS) * g


def _head_lane_mask(h):
    lane = lax.broadcasted_iota(jnp.int32, (1, MEM_WIDTH), 1)
    return ((lane >= h * HEAD_DIM) & (lane < (h + 1) * HEAD_DIM)).astype(F32)


def _mem_attention(q_mem, head_ones, qg_scaled, kexp, vexp):
    qn = _head_rms(q_mem, head_ones, qg_scaled).astype(BF16)
    s = _dot_nt(qn, kexp)
    m_len = kexp.shape[0] // MEM_HEADS
    ps = []
    for h in range(MEM_HEADS):
        sh = s[:, h * m_len:(h + 1) * m_len]
        e = jnp.exp(sh - jnp.max(sh, axis=-1, keepdims=True))
        ps.append((e / jnp.sum(e, axis=-1, keepdims=True)).astype(BF16))
    return _dot(jnp.concatenate(ps, axis=-1), vexp)


def _mem_kv_kernel(mem_ref, g_ref, w_ref, kg_ref, ones_ref, kexp_ref, vexp_ref):
    x = mem_ref[0]
    hn = _rms_rows(x, g_ref[...]).astype(BF16)
    kv = _dot(hn, w_ref[...])
    k = _head_rms(kv[:, :MEM_WIDTH], ones_ref[...], kg_ref[...])
    v = kv[:, MEM_WIDTH:]
    m_len = x.shape[0]
    for h in range(MEM_HEADS):
        mask = _head_lane_mask(h)
        kexp_ref[0, h * m_len:(h + 1) * m_len, :] = (k * mask).astype(BF16)
        vexp_ref[0, h * m_len:(h + 1) * m_len, :] = (v * mask).astype(BF16)


def _mem_kv(mem, g, w_bf, kg_tiled, head_ones):
    b, m_len, d = mem.shape
    out = jax.ShapeDtypeStruct((b, MEM_HEADS * m_len, MEM_WIDTH), BF16)
    const = lambda shape: pl.BlockSpec(shape, lambda i: (0,) * len(shape))
    return pl.pallas_call(
        _mem_kv_kernel,
        out_shape=(out, out),
        grid=(b,),
        in_specs=[pl.BlockSpec((1, m_len, d), lambda i: (i, 0, 0)),
                  const((1, d)), const(w_bf.shape), const((1, MEM_WIDTH)), const(head_ones.shape)],
        out_specs=(pl.BlockSpec((1, MEM_HEADS * m_len, MEM_WIDTH), lambda i: (i, 0, 0)),) * 2,
        name="mem_kv",
    )(mem, g.reshape(1, d), w_bf, kg_tiled, head_ones)


def _conv_mixer_kernel(x_ref, g_ref, win_ref, bglu_ref, dww_ref, dwb_ref, lng_ref, lnb_ref,
                       mqg_ref, ones_ref, kexp_ref, vexp_ref, wout_ref, o_ref, ubuf, cbuf):
    tm = x_ref.shape[1]
    cc = dww_ref.shape[1]

    @pl.when(pl.program_id(1) == 0)
    def _():
        ubuf[...] = jnp.zeros_like(ubuf)

    @pl.when(pl.program_id(1) > 0)
    def _():
        ubuf[:, 0:CONV_HALO, :] = ubuf[:, tm:tm + CONV_HALO, :]

    x = x_ref[0]
    hn = _rms_rows(x, g_ref[...]).astype(BF16)
    proj = _dot(hn, win_ref[...])
    a = proj[:, :cc] + bglu_ref[:, :cc]
    gate = proj[:, cc:2 * cc] + bglu_ref[:, cc:]
    u = a * jax.nn.sigmoid(gate)
    for r in range(SUBLANES):
        ubuf[r, CONV_HALO - r:CONV_HALO - r + tm, :] = u

    rows = 32
    base = CONV_HALO - (CONV_WIDTH - 1)
    for r0 in range(0, tm, rows):
        acc = jnp.broadcast_to(dwb_ref[...], (rows, cc))
        for k in range(CONV_WIDTH):
            r = (base + k) % SUBLANES
            lo = r0 + base + k - r
            acc = acc + ubuf[r, lo:lo + rows, :] * dww_ref[k:k + 1, :]
        mu = jnp.mean(acc, axis=-1, keepdims=True)
        dlt = acc - mu
        var = jnp.mean(dlt * dlt, axis=-1, keepdims=True)
        y = dlt * lax.rsqrt(var + LN_EPS) * lng_ref[...] + lnb_ref[...]
        cbuf[r0:r0 + rows, :] = (y * jax.nn.sigmoid(y)).astype(BF16)

    m = _mem_attention(proj[:, 2 * cc:], ones_ref[...], mqg_ref[...], kexp_ref[0], vexp_ref[0])
    out = _dot(cbuf[...], wout_ref[0:cc, :]) + _dot(m.astype(BF16), wout_ref[cc:, :])
    o_ref[0] = x + out


def _conv_mixer(h, g, win_bf, bglu, dww, dwb, lng, lnb, mqg_scaled, head_ones, kexp, vexp, wout_bf):
    b, s, d = h.shape
    cc = dww.shape[1]
    tm = TM_MIX
    const = lambda shape: pl.BlockSpec(shape, lambda i, j: (0,) * len(shape))
    per_batch = lambda arr: pl.BlockSpec((1,) + arr.shape[1:], lambda i, j: (i, 0, 0))
    return pl.pallas_call(
        _conv_mixer_kernel,
        out_shape=jax.ShapeDtypeStruct(h.shape, F32),
        grid=(b, s // tm),
        in_specs=[pl.BlockSpec((1, tm, d), lambda i, j: (i, j, 0)),
                  const((1, d)), const(win_bf.shape), const((1, 2 * cc)), const(dww.shape),
                  const((1, cc)), const((1, cc)), const((1, cc)), const((1, MEM_WIDTH)),
                  const(head_ones.shape), per_batch(kexp), per_batch(vexp), const(wout_bf.shape)],
        out_specs=pl.BlockSpec((1, tm, d), lambda i, j: (i, j, 0)),
        scratch_shapes=[pltpu.VMEM((SUBLANES, CONV_HALO + tm, cc), F32), pltpu.VMEM((tm, cc), BF16)],
        compiler_params=pltpu.CompilerParams(dimension_semantics=("arbitrary", "arbitrary"),
                                             vmem_limit_bytes=VMEM_LIMIT),
        name="conv_mixer",
    )(h, g.reshape(1, d), win_bf, bglu.reshape(1, -1), dww, dwb.reshape(1, cc), lng.reshape(1, cc),
      lnb.reshape(1, cc), mqg_scaled, head_ones, kexp, vexp, wout_bf)


def _dense_ffn_kernel(x_ref, g_ref, wg_ref, wu_ref, wd_ref, o_ref):
    x = x_ref[...]
    hn = _rms_rows(x, g_ref[...]).astype(BF16)
    f = wg_ref.shape[1]
    out = x
    for lo, hi in ((0, FFN_SPLIT), (FFN_SPLIT, f)):
        gte = _dot(hn, wg_ref[:, lo:hi])
        up = _dot(hn, wu_ref[:, lo:hi])
        act = (gte * jax.nn.sigmoid(gte) * up).astype(BF16)
        out = out + _dot(act, wd_ref[lo:hi, :])
    o_ref[...] = out


def _dense_ffn(h2d, g, wg_bf, wu_bf, wd_bf):
    t, d = h2d.shape
    tm = TM_FFN
    resident = lambda shape: pl.BlockSpec(shape, lambda i: (0, 0), pipeline_mode=pl.Buffered(1))
    return pl.pallas_call(
        _dense_ffn_kernel,
        out_shape=jax.ShapeDtypeStruct((t, d), F32),
        grid=(t // tm,),
        in_specs=[pl.BlockSpec((tm, d), lambda i: (i, 0)), resident((1, d)),
                  resident(wg_bf.shape), resident(wu_bf.shape), resident(wd_bf.shape)],
        out_specs=pl.BlockSpec((tm, d), lambda i: (i, 0)),
        compiler_params=pltpu.CompilerParams(dimension_semantics=("arbitrary",),
                                             vmem_limit_bytes=VMEM_LIMIT),
        name="dense_ffn",
    )(h2d, g.reshape(1, d), wg_bf, wu_bf, wd_bf)


def _alibi_slope(head):
    n_heads = SWA_KV_HEADS * SWA_GROUP
    return 2.0 ** (-ALIBI_MAX_BIAS * (head + 1.0) / n_heads)


def _swa_mixer_kernel(sinks_ref, x_ref, g_ref, win_ref, qg_ref, kg_ref, mqg_ref, ones_ref,
                      kexp_ref, vexp_ref, wout_ref, o_ref, kbuf, vbuf, obuf):
    tm = x_ref.shape[1]
    blk = SWA_BLOCK
    kvw = SWA_KV_HEADS * HEAD_DIM
    qw = SWA_GROUP * kvw
    first_tile = pl.program_id(1) == 0

    @pl.when(first_tile)
    def _():
        kbuf[:, 0:blk, :] = jnp.zeros((SWA_KV_HEADS, blk, kvw), BF16)
        vbuf[:, 0:blk, :] = jnp.zeros((SWA_KV_HEADS, blk, kvw), BF16)

    x = x_ref[0]
    hn = _rms_rows(x, g_ref[...]).astype(BF16)
    proj = _dot(hn, win_ref[...])
    ones = ones_ref[...]
    kn = _head_rms(proj[:, qw:qw + kvw], ones, kg_ref[...])
    v = proj[:, qw + kvw:qw + 2 * kvw]
    for h in range(SWA_KV_HEADS):
        mask = _head_lane_mask(h)
        kbuf[h, blk:blk + tm, :] = (kn * mask).astype(BF16)
        vbuf[h, blk:blk + tm, :] = (v * mask).astype(BF16)

    qi = lax.broadcasted_iota(jnp.int32, (blk, 2 * blk), 0)
    ci = lax.broadcasted_iota(jnp.int32, (blk, 2 * blk), 1)
    dist_i = qi + blk - ci
    dist = dist_i.astype(F32)
    win = (dist_i >= 0) & (dist_i < blk)
    win_first = win & ((ci >= blk) | jnp.logical_not(first_tile))

    for g in range(SWA_GROUP):
        qn = _head_rms(proj[:, g * kvw:(g + 1) * kvw], ones, qg_ref[...]).astype(BF16)
        for n in range(tm // blk):
            mask = win_first if n == 0 else win
            qb = qn[n * blk:(n + 1) * blk, :]
            og = jnp.zeros((blk, kvw), F32)
            for h in range(SWA_KV_HEADS):
                head = h * SWA_GROUP + g
                s = _dot_nt(qb, kbuf[h, n * blk:(n + 2) * blk, :])
                s = jnp.where(mask, s - _alibi_slope(head) * dist, NEG_INF)
                sink = sinks_ref[head]
                mx = jnp.maximum(jnp.max(s, axis=-1, keepdims=True), sink)
                e = jnp.exp(s - mx)
                den = jnp.sum(e, axis=-1, keepdims=True) + jnp.exp(sink - mx)
                og = og + _dot((e / den).astype(BF16), vbuf[h, n * blk:(n + 2) * blk, :])
            obuf[n * blk:(n + 1) * blk, g * kvw:(g + 1) * kvw] = og.astype(BF16)

    kbuf[:, 0:blk, :] = kbuf[:, tm:tm + blk, :]
    vbuf[:, 0:blk, :] = vbuf[:, tm:tm + blk, :]

    m = _mem_attention(proj[:, qw + 2 * kvw:], ones, mqg_ref[...], kexp_ref[0], vexp_ref[0])
    out = _dot(obuf[...], wout_ref[0:qw, :]) + _dot(m.astype(BF16), wout_ref[qw:, :])
    o_ref[0] = x + out


def _swa_mixer(h, g, win_bf, qg_scaled, kg_tiled, sinks, mqg_scaled, head_ones, kexp, vexp, wout_bf):
    b, s, d = h.shape
    tm = TM_MIX
    kvw = SWA_KV_HEADS * HEAD_DIM
    const = lambda shape: pl.BlockSpec(shape, lambda i, j, sk: (0,) * len(shape))
    per_batch = lambda arr: pl.BlockSpec((1,) + arr.shape[1:], lambda i, j, sk: (i, 0, 0))
    grid_spec = pltpu.PrefetchScalarGridSpec(
        num_scalar_prefetch=1,
        grid=(b, s // tm),
        in_specs=[pl.BlockSpec((1, tm, d), lambda i, j, sk: (i, j, 0)),
                  const((1, d)), const(win_bf.shape), const((1, kvw)), const((1, kvw)),
                  const((1, MEM_WIDTH)), const(head_ones.shape), per_batch(kexp), per_batch(vexp),
                  const(wout_bf.shape)],
        out_specs=pl.BlockSpec((1, tm, d), lambda i, j, sk: (i, j, 0)),
        scratch_shapes=[pltpu.VMEM((SWA_KV_HEADS, SWA_BLOCK + tm, kvw), BF16),
                        pltpu.VMEM((SWA_KV_HEADS, SWA_BLOCK + tm, kvw), BF16),
                        pltpu.VMEM((tm, SWA_GROUP * kvw), BF16)])
    return pl.pallas_call(
        _swa_mixer_kernel,
        out_shape=jax.ShapeDtypeStruct(h.shape, F32),
        grid_spec=grid_spec,
        compiler_params=pltpu.CompilerParams(dimension_semantics=("arbitrary", "arbitrary"),
                                             vmem_limit_bytes=VMEM_LIMIT),
        name="swa_mixer",
    )(sinks, h, g.reshape(1, d), win_bf, qg_scaled, kg_tiled, mqg_scaled, head_ones, kexp, vexp, wout_bf)


def _router_kernel(x_ref, g_ref, r_ref, info_ref, cnt_ref, run):
    tm = x_ref.shape[0]

    @pl.when(pl.program_id(0) == 0)
    def _():
        run[...] = jnp.zeros_like(run)

    hn = _rms_rows(x_ref[...], g_ref[...])
    logits = jnp.dot(hn, r_ref[...], precision=lax.Precision.HIGHEST, preferred_element_type=F32)
    lane = lax.broadcasted_iota(jnp.int32, (tm, LANES), 1).astype(F32)
    logits = jnp.where(lane < N_EXPERTS, logits, -jnp.inf)
    m1 = jnp.max(logits, axis=-1, keepdims=True)
    e1 = jnp.min(jnp.where(logits == m1, lane, float(LANES)), axis=-1, keepdims=True)
    oh1 = lane == e1
    rest = jnp.where(oh1, -jnp.inf, logits)
    m2 = jnp.max(rest, axis=-1, keepdims=True)
    e2 = jnp.min(jnp.where(rest == m2, lane, float(LANES)), axis=-1, keepdims=True)
    oh2 = lane == e2
    z = jnp.exp(m2 - m1)
    w1 = 1.0 / (1.0 + z)
    w2 = z / (1.0 + z)

    both = jnp.where(oh1 | oh2, 1.0, 0.0)
    ri = lax.broadcasted_iota(jnp.int32, (tm, tm), 0)
    ci = lax.broadcasted_iota(jnp.int32, (tm, tm), 1)
    before = jnp.where(ci < ri, 1.0, 0.0).astype(BF16)
    cnt = _dot(before, both.astype(BF16)) + run[...]
    r1 = jnp.sum(jnp.where(oh1, cnt, 0.0), axis=-1, keepdims=True)
    r2 = jnp.sum(jnp.where(oh2, cnt, 0.0), axis=-1, keepdims=True)
    run[...] = run[...] + jnp.sum(both, axis=0, keepdims=True)
    cnt_ref[...] = run[...]

    info = jnp.zeros((tm, LANES), F32)
    for idx, val in enumerate((e1, e2, r1, r2, w1, w2)):
        info = jnp.where(lane == idx, val, info)
    info_ref[...] = info


def _router(h2d, g, router_pad):
    t, d = h2d.shape
    tm = TM_ROUTE
    const = lambda shape: pl.BlockSpec(shape, lambda i: (0, 0))
    return pl.pallas_call(
        _router_kernel,
        out_shape=(jax.ShapeDtypeStruct((t, LANES), F32), jax.ShapeDtypeStruct((1, LANES), F32)),
        grid=(t // tm,),
        in_specs=[pl.BlockSpec((tm, d), lambda i: (i, 0)), const((1, d)), const(router_pad.shape)],
        out_specs=(pl.BlockSpec((tm, LANES), lambda i: (i, 0)), const((1, LANES))),
        scratch_shapes=[pltpu.VMEM((1, LANES), F32)],
        compiler_params=pltpu.CompilerParams(dimension_semantics=("arbitrary",)),
        name="router",
    )(h2d, g.reshape(1, d), router_pad)


def _row_copy(src, src_row, dst, dst_row, sem):
    return pltpu.make_async_copy(src.at[pl.ds(src_row, 1)], dst.at[pl.ds(dst_row, 1)], sem)


def _dispatch_kernel(slot_ref, end_ref, x_ref, g_ref, xs_hbm, hbuf, zbuf, sem, zsem):
    tm = x_ref.shape[0]
    ts = zbuf.shape[0]
    i = pl.program_id(0)

    @pl.when(i == 0)
    def _():
        zbuf[...] = jnp.zeros_like(zbuf)
        for e in range(N_EXPERTS):
            end = end_ref[e]
            end_up = pl.multiple_of(lax.shift_left(lax.shift_right_logical(end + (SUBLANES - 1), 3), 3), SUBLANES)
            block = pltpu.make_async_copy(zbuf, xs_hbm.at[pl.ds(end_up, ts)], zsem)
            block.start()
            block.wait()
            for q in range(SUBLANES - 1):
                @pl.when(end + q < end_up)
                def _():
                    row = _row_copy(zbuf, 0, xs_hbm, end + q, zsem)
                    row.start()
                    row.wait()
        n_tiles = xs_hbm.shape[0] // ts
        for q in range(n_tiles - slot_ref.shape[0] // ts):
            tile = end_ref[N_EXPERTS] + q

            @pl.when(tile < n_tiles)
            def _():
                block = pltpu.make_async_copy(zbuf, xs_hbm.at[pl.ds(pl.multiple_of(tile * ts, ts), ts)], zsem)
                block.start()
                block.wait()

    hbuf[...] = _rms_rows(x_ref[...], g_ref[...])

    def issue(r, c):
        for k in range(2):
            _row_copy(hbuf, r, xs_hbm, slot_ref[2 * (i * tm + r) + k], sem).start()
        return c
    lax.fori_loop(0, tm, issue, 0, unroll=8)

    def wait(r, c):
        for k in range(2):
            _row_copy(hbuf, r, xs_hbm, 0, sem).wait()
        return c
    lax.fori_loop(0, tm, wait, 0, unroll=8)


def _dispatch(slots, ends, h2d, g, n_rows):
    t, d = h2d.shape
    tm = TM_ROUTE
    grid_spec = pltpu.PrefetchScalarGridSpec(
        num_scalar_prefetch=2,
        grid=(t // tm,),
        in_specs=[pl.BlockSpec((tm, d), lambda i, sl, cn: (i, 0)), pl.BlockSpec((1, d), lambda i, sl, cn: (0, 0))],
        out_specs=pl.BlockSpec(memory_space=pl.ANY),
        scratch_shapes=[pltpu.VMEM((tm, d), F32), pltpu.VMEM((TS_MOE, d), F32),
                        pltpu.SemaphoreType.DMA, pltpu.SemaphoreType.DMA])
    return pl.pallas_call(
        _dispatch_kernel,
        out_shape=jax.ShapeDtypeStruct((n_rows, d), F32),
        grid_spec=grid_spec,
        compiler_params=pltpu.CompilerParams(dimension_semantics=("arbitrary",)),
        name="moe_dispatch",
    )(slots, ends, h2d, g.reshape(1, d))


def _moe_ffn_kernel(te_ref, nvt_ref, xs_ref, wg_ref, wu_ref, wd_ref, y_ref):
    @pl.when(pl.program_id(0) >= nvt_ref[0])
    def _():
        y_ref[...] = jnp.zeros_like(y_ref)

    @pl.when(pl.program_id(0) < nvt_ref[0])
    def _():
        x = xs_ref[...].astype(BF16)
        f = wg_ref.shape[2]
        out = None
        for lo in range(0, f, TF_MOE):
            gte = _dot(x, wg_ref[0, :, lo:lo + TF_MOE])
            up = _dot(x, wu_ref[0, :, lo:lo + TF_MOE])
            act = (gte * jax.nn.sigmoid(gte) * up).astype(BF16)
            part = _dot(act, wd_ref[0, lo:lo + TF_MOE, :])
            out = part if out is None else out + part
        y_ref[...] = out


def _moe_ffn(tile_expert, n_valid_tiles, xs, wg_bf, wu_bf, wd_bf):
    d = xs.shape[1]
    n_exp, _, f = wg_bf.shape
    ts = TS_MOE
    n_tiles = tile_expert.shape[0]
    expert_w = lambda shape: pl.BlockSpec((1,) + shape, lambda i, te, nvt: (te[i], 0, 0),
                                          pipeline_mode=pl.Buffered(1))
    grid_spec = pltpu.PrefetchScalarGridSpec(
        num_scalar_prefetch=2,
        grid=(n_tiles,),
        in_specs=[pl.BlockSpec((ts, d), lambda i, te, nvt: (i, 0)),
                  expert_w((d, f)), expert_w((d, f)), expert_w((f, d))],
        out_specs=pl.BlockSpec((ts, d), lambda i, te, nvt: (i, 0)))
    return pl.pallas_call(
        _moe_ffn_kernel,
        out_shape=jax.ShapeDtypeStruct((n_tiles * ts, d), F32),
        grid_spec=grid_spec,
        compiler_params=pltpu.CompilerParams(dimension_semantics=("arbitrary",),
                                             vmem_limit_bytes=VMEM_LIMIT),
        name="moe_ffn",
    )(tile_expert, n_valid_tiles, xs, wg_bf, wu_bf, wd_bf)


def _combine_kernel(slot_ref, h_ref, info_ref, y_hbm, o_ref, ybuf, sem):
    tm, d = h_ref.shape
    i = pl.program_id(0)
    n = pl.num_programs(0)

    def fetch(tile, start):
        buf = tile % 2

        def body(r, c):
            for k in range(2):
                cp = _row_copy(y_hbm, slot_ref[2 * (tile * tm + r) + k] if start else 0, ybuf.at[buf, k], r,
                               sem.at[buf])
                cp.start() if start else cp.wait()
            return c
        lax.fori_loop(0, tm, body, 0, unroll=8)

    @pl.when(i == 0)
    def _():
        fetch(i, True)

    @pl.when(i + 1 < n)
    def _():
        fetch(i + 1, True)

    fetch(i, False)
    buf = i % 2
    w1 = info_ref[:, 4:5]
    w2 = info_ref[:, 5:6]
    o_ref[...] = h_ref[...] + (w1 * ybuf[buf, 0] + w2 * ybuf[buf, 1])


def _combine(slots, h2d, info, y):
    t, d = h2d.shape
    tm = TM_ROUTE
    grid_spec = pltpu.PrefetchScalarGridSpec(
        num_scalar_prefetch=1,
        grid=(t // tm,),
        in_specs=[pl.BlockSpec((tm, d), lambda i, sl: (i, 0)), pl.BlockSpec((tm, LANES), lambda i, sl: (i, 0)),
                  pl.BlockSpec(memory_space=pl.ANY)],
        out_specs=pl.BlockSpec((tm, d), lambda i, sl: (i, 0)),
        scratch_shapes=[pltpu.VMEM((2, 2, tm, d), F32), pltpu.SemaphoreType.DMA((2,))])
    return pl.pallas_call(
        _combine_kernel,
        out_shape=jax.ShapeDtypeStruct((t, d), F32),
        grid_spec=grid_spec,
        compiler_params=pltpu.CompilerParams(dimension_semantics=("arbitrary",)),
        name="moe_combine",
    )(slots, h2d, info, y)


def _moe(h2d, g, router, wg_bf, wu_bf, wd_bf):
    t, d = h2d.shape
    ts = TS_MOE
    router_pad = jnp.pad(router, ((0, 0), (0, LANES - N_EXPERTS)))
    info, counts = _router(h2d, g, router_pad)

    counts = counts[0, :N_EXPERTS].astype(jnp.int32)
    tiles_per_expert = (counts + ts - 1) // ts
    tile_end = jnp.cumsum(tiles_per_expert)
    starts = (tile_end - tiles_per_expert) * ts
    n_valid = tile_end[-1]
    chosen = info[:, 0:2].astype(jnp.int32)[..., None] == jnp.arange(N_EXPERTS, dtype=jnp.int32)
    slots = (jnp.sum(jnp.where(chosen, starts, 0), axis=-1) + info[:, 2:4].astype(jnp.int32)).reshape(-1)
    n_tiles = 2 * t // ts + N_EXPERTS
    tile = jnp.minimum(jnp.arange(n_tiles, dtype=jnp.int32), n_valid - 1)
    tile_expert = jnp.sum((tile[:, None] >= tile_end[None, :]).astype(jnp.int32), axis=1)
    ends = jnp.concatenate([starts + counts, n_valid.reshape(1)])

    xs = _dispatch(slots, ends, h2d, g, (n_tiles + 1) * ts)
    y = _moe_ffn(tile_expert, n_valid.reshape(1), xs, wg_bf, wu_bf, wd_bf)
    return _combine(slots, h2d, info, y)


def _regroup_heads(w, axis):
    shape = w.shape
    split = shape[:axis] + (SWA_KV_HEADS, SWA_GROUP, HEAD_DIM) + shape[axis + 1:]
    return jnp.swapaxes(w.reshape(split), axis, axis + 1).reshape(shape)


def kernel(x, mem, mem_norm_g, w_mem_kv, mem_k_norm_g, cv_attn_norm_g, cv_w_in, cv_b_glu, cv_dw_w, cv_dw_b, cv_ln_g, cv_ln_b, cv_memq_norm_g, cv_w_out, cv_ffn_norm_g, cv_w_gate, cv_w_up, cv_w_down, sw_attn_norm_g, sw_w_in, sw_q_norm_g, sw_k_norm_g, sw_sinks, sw_memq_norm_g, sw_w_out, sw_ffn_norm_g, sw_router, sw_we_gate, sw_we_up, sw_we_down):
    b, s, d = x.shape
    scale = HEAD_DIM ** -0.5
    lane_head = jnp.arange(MEM_WIDTH) // HEAD_DIM
    head_ones = (lane_head[:, None] == lane_head[None, :]).astype(BF16)
    tile4 = lambda gvec: jnp.tile(gvec, MEM_HEADS).reshape(1, MEM_WIDTH)

    kexp, vexp = _mem_kv(mem, mem_norm_g, w_mem_kv.astype(BF16), tile4(mem_k_norm_g), head_ones)

    h = x
    depth = cv_w_in.shape[0] + sw_w_in.shape[0]
    for i in range(depth):
        j = i // 2
        if i % 2 == 0:
            h = _conv_mixer(h, cv_attn_norm_g[j], cv_w_in[j].astype(BF16), cv_b_glu[j], cv_dw_w[j], cv_dw_b[j],
                            cv_ln_g[j], cv_ln_b[j], tile4(cv_memq_norm_g[j]) * scale, head_ones, kexp, vexp,
                            cv_w_out[j].astype(BF16))
            h = _dense_ffn(h.reshape(b * s, d), cv_ffn_norm_g[j], cv_w_gate[j].astype(BF16),
                           cv_w_up[j].astype(BF16), cv_w_down[j].astype(BF16)).reshape(b, s, d)
        else:
            qw = SWA_KV_HEADS * SWA_GROUP * HEAD_DIM
            w_in = jnp.concatenate([_regroup_heads(sw_w_in[j][:, :qw], 1), sw_w_in[j][:, qw:]], axis=1)
            w_out = jnp.concatenate([_regroup_heads(sw_w_out[j][:qw], 0), sw_w_out[j][qw:]], axis=0)
            h = _swa_mixer(h, sw_attn_norm_g[j], w_in.astype(BF16), tile4(sw_q_norm_g[j]) * scale,
                           tile4(sw_k_norm_g[j]), sw_sinks[j], tile4(sw_memq_norm_g[j]) * scale, head_ones,
                           kexp, vexp, w_out.astype(BF16))
            h = _moe(h.reshape(b * s, d), sw_ffn_norm_g[j], sw_router[j], sw_we_gate[j].astype(BF16),
                     sw_we_up[j].astype(BF16), sw_we_down[j].astype(BF16)).reshape(b, s, d)
    return h
```

```python
import jax
import jax.numpy as jnp
from jax import lax
from jax.experimental import pallas as pl
from jax.experimental.pallas import tpu as pltpu

F32 = jnp.float32
BF16 = jnp.bfloat16

HEAD_DIM = 64
MEM_HEADS = 4
MEM_WIDTH = MEM_HEADS * HEAD_DIM
CONV_WIDTH = 31
CONV_HALO = 32
SWA_KV_HEADS = 4
SWA_GROUP = 3
SWA_BLOCK = 128
N_EXPERTS = 8
RMS_EPS = 1e-6
LN_EPS = 1e-5
ALIBI_MAX_BIAS = 8.0
NEG_INF = -1e30
LANES = 128
SUBLANES = 8
VMEM_LIMIT = 56 * 1024 * 1024

TM_MIX = 512
TM_FFN = 512
FFN_SPLIT = 1536
TM_ROUTE = 512
TS_MOE = 512
TF_MOE = 1792
W_STAGE_COLS = 256
W_STAGE_SLOTS = 3


def _dot(a, b):
    return jnp.dot(a, b, preferred_element_type=F32)


def _dot_nt(a, b):
    return lax.dot_general(a, b, (((1,), (1,)), ((), ())), preferred_element_type=F32)


def _rms_rows(x, g):
    ms = jnp.mean(x * x, axis=-1, keepdims=True)
    return x * lax.rsqrt(ms + RMS_EPS) * g


def _split_bf16(x):
    hi = x.astype(BF16)
    return hi, (x - hi.astype(F32)).astype(BF16)


def _head_mean_sq(x, head_ones):
    hi, lo = _split_bf16(x * x)
    return (_dot(hi, head_ones) + _dot(lo, head_ones)) * (1.0 / HEAD_DIM)


def _head_rms(x, head_ones, g):
    return x * lax.rsqrt(_head_mean_sq(x, head_ones) + RMS_EPS) * g


def _head_lane_mask(h):
    lane = lax.broadcasted_iota(jnp.int32, (1, MEM_WIDTH), 1)
    return ((lane >= h * HEAD_DIM) & (lane < (h + 1) * HEAD_DIM)).astype(F32)


def _mem_attention(q_mem, head_ones, qg_scaled, kexp, vexp):
    qn = _head_rms(q_mem, head_ones, qg_scaled).astype(BF16)
    s = _dot_nt(qn, kexp)
    m_len = kexp.shape[0] // MEM_HEADS
    ps = []
    for h in range(MEM_HEADS):
        sh = s[:, h * m_len:(h + 1) * m_len]
        e = jnp.exp(sh - jnp.max(sh, axis=-1, keepdims=True))
        ps.append((e / jnp.sum(e, axis=-1, keepdims=True)).astype(BF16))
    return _dot(jnp.concatenate(ps, axis=-1), vexp)


def _mem_kv_kernel(mem_ref, g_ref, w_ref, kg_ref, ones_ref, kexp_ref, vexp_ref):
    x = mem_ref[0]
    hn = _rms_rows(x, g_ref[...]).astype(BF16)
    kv = _dot(hn, w_ref[...])
    k = _head_rms(kv[:, :MEM_WIDTH], ones_ref[...], kg_ref[...])
    v = kv[:, MEM_WIDTH:]
    m_len = x.shape[0]
    for h in range(MEM_HEADS):
        mask = _head_lane_mask(h)
        kexp_ref[0, h * m_len:(h + 1) * m_len, :] = (k * mask).astype(BF16)
        vexp_ref[0, h * m_len:(h + 1) * m_len, :] = (v * mask).astype(BF16)


def _mem_kv(mem, g, w_bf, kg_tiled, head_ones):
    b, m_len, d = mem.shape
    out = jax.ShapeDtypeStruct((b, MEM_HEADS * m_len, MEM_WIDTH), BF16)
    const = lambda shape: pl.BlockSpec(shape, lambda i: (0,) * len(shape))
    return pl.pallas_call(
        _mem_kv_kernel,
        out_shape=(out, out),
        grid=(b,),
        in_specs=[pl.BlockSpec((1, m_len, d), lambda i: (i, 0, 0)),
                  const((1, d)), const(w_bf.shape), const((1, MEM_WIDTH)), const(head_ones.shape)],
        out_specs=(pl.BlockSpec((1, MEM_HEADS * m_len, MEM_WIDTH), lambda i: (i, 0, 0)),) * 2,
        name="mem_kv",
    )(mem, g.reshape(1, d), w_bf, kg_tiled, head_ones)


def _conv_mixer_kernel(x_ref, g_ref, win_ref, bglu_ref, dww_ref, dwb_ref, lng_ref, lnb_ref,
                       mqg_ref, ones_ref, kexp_ref, vexp_ref, wout_ref, o_ref, ubuf, cbuf):
    tm = x_ref.shape[1]
    cc = dww_ref.shape[1]

    @pl.when(pl.program_id(1) == 0)
    def _():
        ubuf[...] = jnp.zeros_like(ubuf)

    @pl.when(pl.program_id(1) > 0)
    def _():
        ubuf[:, 0:CONV_HALO, :] = ubuf[:, tm:tm + CONV_HALO, :]

    x = x_ref[0]
    hn = _rms_rows(x, g_ref[...]).astype(BF16)
    proj = _dot(hn, win_ref[...])
    a = proj[:, :cc] + bglu_ref[:, :cc]
    gate = proj[:, cc:2 * cc] + bglu_ref[:, cc:]
    u = a * jax.nn.sigmoid(gate)
    for r in range(SUBLANES):
        ubuf[r, CONV_HALO - r:CONV_HALO - r + tm, :] = u

    rows = 32
    base = CONV_HALO - (CONV_WIDTH - 1)
    for r0 in range(0, tm, rows):
        acc = jnp.broadcast_to(dwb_ref[...], (rows, cc))
        for k in range(CONV_WIDTH):
            r = (base + k) % SUBLANES
            lo = r0 + base + k - r
            acc = acc + ubuf[r, lo:lo + rows, :] * dww_ref[k:k + 1, :]
        mu = jnp.mean(acc, axis=-1, keepdims=True)
        dlt = acc - mu
        var = jnp.mean(dlt * dlt, axis=-1, keepdims=True)
        y = dlt * lax.rsqrt(var + LN_EPS) * lng_ref[...] + lnb_ref[...]
        cbuf[r0:r0 + rows, :] = (y * jax.nn.sigmoid(y)).astype(BF16)

    m = _mem_attention(proj[:, 2 * cc:], ones_ref[...], mqg_ref[...], kexp_ref[0], vexp_ref[0])
    out = _dot(cbuf[...], wout_ref[0:cc, :]) + _dot(m.astype(BF16), wout_ref[cc:, :])
    o_ref[0] = x + out


def _conv_mixer(h, g, win_bf, bglu, dww, dwb, lng, lnb, mqg_scaled, head_ones, kexp, vexp, wout_bf):
    b, s, d = h.shape
    cc = dww.shape[1]
    tm = TM_MIX
    const = lambda shape: pl.BlockSpec(shape, lambda i, j: (0,) * len(shape))
    per_batch = lambda arr: pl.BlockSpec((1,) + arr.shape[1:], lambda i, j: (i, 0, 0))
    return pl.pallas_call(
        _conv_mixer_kernel,
        out_shape=jax.ShapeDtypeStruct(h.shape, F32),
        grid=(b, s // tm),
        in_specs=[pl.BlockSpec((1, tm, d), lambda i, j: (i, j, 0)),
                  const((1, d)), const(win_bf.shape), const((1, 2 * cc)), const(dww.shape),
                  const((1, cc)), const((1, cc)), const((1, cc)), const((1, MEM_WIDTH)),
                  const(head_ones.shape), per_batch(kexp), per_batch(vexp), const(wout_bf.shape)],
        out_specs=pl.BlockSpec((1, tm, d), lambda i, j: (i, j, 0)),
        scratch_shapes=[pltpu.VMEM((SUBLANES, CONV_HALO + tm, cc), F32), pltpu.VMEM((tm, cc), BF16)],
        compiler_params=pltpu.CompilerParams(dimension_semantics=("arbitrary", "arbitrary"),
                                             vmem_limit_bytes=VMEM_LIMIT),
        name="conv_mixer",
    )(h, g.reshape(1, d), win_bf, bglu.reshape(1, -1), dww, dwb.reshape(1, cc), lng.reshape(1, cc),
      lnb.reshape(1, cc), mqg_scaled, head_ones, kexp, vexp, wout_bf)


def _dense_ffn_kernel(x_ref, g_ref, wg_ref, wu_ref, wd_ref, o_ref):
    x = x_ref[...]
    hn = _rms_rows(x, g_ref[...]).astype(BF16)
    f = wg_ref.shape[1]
    out = x
    for lo, hi in ((0, FFN_SPLIT), (FFN_SPLIT, f)):
        gte = _dot(hn, wg_ref[:, lo:hi])
        up = _dot(hn, wu_ref[:, lo:hi])
        act = (gte * jax.nn.sigmoid(gte) * up).astype(BF16)
        out = out + _dot(act, wd_ref[lo:hi, :])
    o_ref[...] = out


def _dense_ffn(h2d, g, wg_bf, wu_bf, wd_bf):
    t, d = h2d.shape
    tm = TM_FFN
    resident = lambda shape: pl.BlockSpec(shape, lambda i: (0, 0), pipeline_mode=pl.Buffered(1))
    return pl.pallas_call(
        _dense_ffn_kernel,
        out_shape=jax.ShapeDtypeStruct((t, d), F32),
        grid=(t // tm,),
        in_specs=[pl.BlockSpec((tm, d), lambda i: (i, 0)), resident((1, d)),
                  resident(wg_bf.shape), resident(wu_bf.shape), resident(wd_bf.shape)],
        out_specs=pl.BlockSpec((tm, d), lambda i: (i, 0)),
        compiler_params=pltpu.CompilerParams(dimension_semantics=("arbitrary",),
                                             vmem_limit_bytes=VMEM_LIMIT),
        name="dense_ffn",
    )(h2d, g.reshape(1, d), wg_bf, wu_bf, wd_bf)


def _alibi_slope(head):
    n_heads = SWA_KV_HEADS * SWA_GROUP
    return 2.0 ** (-ALIBI_MAX_BIAS * (head + 1.0) / n_heads)


def _swa_mixer_kernel(sinks_ref, x_ref, g_ref, win_ref, qg_ref, kg_ref, mqg_ref, ones_ref,
                      kexp_ref, vexp_ref, wout_ref, o_ref, kbuf, vbuf, obuf):
    tm = x_ref.shape[1]
    blk = SWA_BLOCK
    kvw = SWA_KV_HEADS * HEAD_DIM
    qw = SWA_GROUP * kvw
    first_tile = pl.program_id(1) == 0

    @pl.when(first_tile)
    def _():
        kbuf[:, 0:blk, :] = jnp.zeros((SWA_KV_HEADS, blk, kvw), BF16)
        vbuf[:, 0:blk, :] = jnp.zeros((SWA_KV_HEADS, blk, kvw), BF16)

    x = x_ref[0]
    hn = _rms_rows(x, g_ref[...]).astype(BF16)
    proj = _dot(hn, win_ref[...])
    ones = ones_ref[...]
    kn = _head_rms(proj[:, qw:qw + kvw], ones, kg_ref[...])
    v = proj[:, qw + kvw:qw + 2 * kvw]
    for h in range(SWA_KV_HEADS):
        mask = _head_lane_mask(h)
        kbuf[h, blk:blk + tm, :] = (kn * mask).astype(BF16)
        vbuf[h, blk:blk + tm, :] = (v * mask).astype(BF16)

    qi = lax.broadcasted_iota(jnp.int32, (blk, 2 * blk), 0)
    ci = lax.broadcasted_iota(jnp.int32, (blk, 2 * blk), 1)
    dist_i = qi + blk - ci
    dist = dist_i.astype(F32)
    win = (dist_i >= 0) & (dist_i < blk)
    win_first = win & ((ci >= blk) | jnp.logical_not(first_tile))

    for g in range(SWA_GROUP):
        qn = _head_rms(proj[:, g * kvw:(g + 1) * kvw], ones, qg_ref[...]).astype(BF16)
        for n in range(tm // blk):
            mask = win_first if n == 0 else win
            qb = qn[n * blk:(n + 1) * blk, :]
            og = jnp.zeros((blk, kvw), F32)
            for h in range(SWA_KV_HEADS):
                head = h * SWA_GROUP + g
                s = _dot_nt(qb, kbuf[h, n * blk:(n + 2) * blk, :])
                s = jnp.where(mask, s - _alibi_slope(head) * dist, NEG_INF)
                sink = sinks_ref[head]
                mx = jnp.maximum(jnp.max(s, axis=-1, keepdims=True), sink)
                e = jnp.exp(s - mx)
                den = jnp.sum(e, axis=-1, keepdims=True) + jnp.exp(sink - mx)
                og = og + _dot((e / den).astype(BF16), vbuf[h, n * blk:(n + 2) * blk, :])
            obuf[n * blk:(n + 1) * blk, g * kvw:(g + 1) * kvw] = og.astype(BF16)

    kbuf[:, 0:blk, :] = kbuf[:, tm:tm + blk, :]
    vbuf[:, 0:blk, :] = vbuf[:, tm:tm + blk, :]

    m = _mem_attention(proj[:, qw + 2 * kvw:], ones, mqg_ref[...], kexp_ref[0], vexp_ref[0])
    out = _dot(obuf[...], wout_ref[0:qw, :]) + _dot(m.astype(BF16), wout_ref[qw:, :])
    o_ref[0] = x + out


def _swa_mixer(h, g, win_bf, qg_scaled, kg_tiled, sinks, mqg_scaled, head_ones, kexp, vexp, wout_bf):
    b, s, d = h.shape
    tm = TM_MIX
    kvw = SWA_KV_HEADS * HEAD_DIM
    const = lambda shape: pl.BlockSpec(shape, lambda i, j, sk: (0,) * len(shape))
    per_batch = lambda arr: pl.BlockSpec((1,) + arr.shape[1:], lambda i, j, sk: (i, 0, 0))
    grid_spec = pltpu.PrefetchScalarGridSpec(
        num_scalar_prefetch=1,
        grid=(b, s // tm),
        in_specs=[pl.BlockSpec((1, tm, d), lambda i, j, sk: (i, j, 0)),
                  const((1, d)), const(win_bf.shape), const((1, kvw)), const((1, kvw)),
                  const((1, MEM_WIDTH)), const(head_ones.shape), per_batch(kexp), per_batch(vexp),
                  const(wout_bf.shape)],
        out_specs=pl.BlockSpec((1, tm, d), lambda i, j, sk: (i, j, 0)),
        scratch_shapes=[pltpu.VMEM((SWA_KV_HEADS, SWA_BLOCK + tm, kvw), BF16),
                        pltpu.VMEM((SWA_KV_HEADS, SWA_BLOCK + tm, kvw), BF16),
                        pltpu.VMEM((tm, SWA_GROUP * kvw), BF16)])
    return pl.pallas_call(
        _swa_mixer_kernel,
        out_shape=jax.ShapeDtypeStruct(h.shape, F32),
        grid_spec=grid_spec,
        compiler_params=pltpu.CompilerParams(dimension_semantics=("arbitrary", "arbitrary"),
                                             vmem_limit_bytes=VMEM_LIMIT),
        name="swa_mixer",
    )(sinks, h, g.reshape(1, d), win_bf, qg_scaled, kg_tiled, mqg_scaled, head_ones, kexp, vexp, wout_bf)


def _router_kernel(x_ref, g_ref, r_ref, info_ref, cnt_ref, run):
    tm = x_ref.shape[0]

    @pl.when(pl.program_id(0) == 0)
    def _():
        run[...] = jnp.zeros_like(run)

    hn_hi, hn_lo = _split_bf16(_rms_rows(x_ref[...], g_ref[...]))
    r_hi, r_lo = _split_bf16(r_ref[...])
    logits = _dot(hn_hi, r_hi) + (_dot(hn_hi, r_lo) + _dot(hn_lo, r_hi))
    lane = lax.broadcasted_iota(jnp.int32, (tm, LANES), 1).astype(F32)
    logits = jnp.where(lane < N_EXPERTS, logits, -jnp.inf)
    m1 = jnp.max(logits, axis=-1, keepdims=True)
    e1 = jnp.min(jnp.where(logits == m1, lane, float(LANES)), axis=-1, keepdims=True)
    oh1 = lane == e1
    rest = jnp.where(oh1, -jnp.inf, logits)
    m2 = jnp.max(rest, axis=-1, keepdims=True)
    e2 = jnp.min(jnp.where(rest == m2, lane, float(LANES)), axis=-1, keepdims=True)
    oh2 = lane == e2
    z = jnp.exp(m2 - m1)
    w1 = 1.0 / (1.0 + z)
    w2 = z / (1.0 + z)

    both = jnp.where(oh1 | oh2, 1.0, 0.0)
    ri = lax.broadcasted_iota(jnp.int32, (tm, tm), 0)
    ci = lax.broadcasted_iota(jnp.int32, (tm, tm), 1)
    before = jnp.where(ci < ri, 1.0, 0.0).astype(BF16)
    cnt = _dot(before, both.astype(BF16)) + run[...]
    r1 = jnp.sum(jnp.where(oh1, cnt, 0.0), axis=-1, keepdims=True)
    r2 = jnp.sum(jnp.where(oh2, cnt, 0.0), axis=-1, keepdims=True)
    run[...] = run[...] + jnp.sum(both, axis=0, keepdims=True)
    cnt_ref[...] = run[...]

    info = jnp.zeros((tm, LANES), F32)
    for idx, val in enumerate((e1, e2, r1, r2, w1, w2)):
        info = jnp.where(lane == idx, val, info)
    info_ref[...] = info


def _router(h2d, g, router_pad):
    t, d = h2d.shape
    tm = TM_ROUTE
    const = lambda shape: pl.BlockSpec(shape, lambda i: (0, 0))
    return pl.pallas_call(
        _router_kernel,
        out_shape=(jax.ShapeDtypeStruct((t, LANES), F32), jax.ShapeDtypeStruct((1, LANES), F32)),
        grid=(t // tm,),
        in_specs=[pl.BlockSpec((tm, d), lambda i: (i, 0)), const((1, d)), const(router_pad.shape)],
        out_specs=(pl.BlockSpec((tm, LANES), lambda i: (i, 0)), const((1, LANES))),
        scratch_shapes=[pltpu.VMEM((1, LANES), F32)],
        compiler_params=pltpu.CompilerParams(dimension_semantics=("arbitrary",)),
        name="router",
    )(h2d, g.reshape(1, d), router_pad)


def _row_copy(src, src_row, dst, dst_row, sem):
    return pltpu.make_async_copy(src.at[pl.ds(src_row, 1)], dst.at[pl.ds(dst_row, 1)], sem)


def _dispatch_kernel(slot_ref, end_ref, x_ref, g_ref, xs_hbm, hbuf, zbuf, sem, zsem):
    tm = x_ref.shape[0]
    ts = zbuf.shape[0]
    i = pl.program_id(0)

    @pl.when(i == 0)
    def _():
        zbuf[...] = jnp.zeros_like(zbuf)
        for e in range(N_EXPERTS):
            end = end_ref[e]
            end_up = pl.multiple_of(lax.shift_left(lax.shift_right_logical(end + (SUBLANES - 1), 3), 3), SUBLANES)
            block = pltpu.make_async_copy(zbuf, xs_hbm.at[pl.ds(end_up, ts)], zsem)
            block.start()
            block.wait()
            for q in range(SUBLANES - 1):
                @pl.when(end + q < end_up)
                def _():
                    row = _row_copy(zbuf, 0, xs_hbm, end + q, zsem)
                    row.start()
                    row.wait()
        n_tiles = xs_hbm.shape[0] // ts
        for q in range(n_tiles - slot_ref.shape[0] // ts):
            tile = end_ref[N_EXPERTS] + q

            @pl.when(tile < n_tiles)
            def _():
                block = pltpu.make_async_copy(zbuf, xs_hbm.at[pl.ds(pl.multiple_of(tile * ts, ts), ts)], zsem)
                block.start()
                block.wait()

    def drain(buf):
        def wait(r, c):
            for k in range(2):
                _row_copy(hbuf.at[buf], r, xs_hbm, 0, sem.at[buf]).wait()
            return c
        lax.fori_loop(0, tm, wait, 0, unroll=8)

    buf = i % 2

    @pl.when(i >= 2)
    def _():
        drain(buf)

    hbuf[buf] = _rms_rows(x_ref[...], g_ref[...])

    def issue(r, c):
        for k in range(2):
            _row_copy(hbuf.at[buf], r, xs_hbm, slot_ref[2 * (i * tm + r) + k], sem.at[buf]).start(priority=k)
        return c
    lax.fori_loop(0, tm, issue, 0, unroll=8)

    n_steps = slot_ref.shape[0] // (2 * tm)

    @pl.when(i == n_steps - 1)
    def _():
        if n_steps > 1:
            drain(1 - buf)
        drain(buf)


def _dispatch(slots, ends, h2d, g, n_rows):
    t, d = h2d.shape
    tm = TM_ROUTE
    grid_spec = pltpu.PrefetchScalarGridSpec(
        num_scalar_prefetch=2,
        grid=(t // tm,),
        in_specs=[pl.BlockSpec((tm, d), lambda i, sl, cn: (i, 0)), pl.BlockSpec((1, d), lambda i, sl, cn: (0, 0))],
        out_specs=pl.BlockSpec(memory_space=pl.ANY),
        scratch_shapes=[pltpu.VMEM((2, tm, d), F32), pltpu.VMEM((TS_MOE, d), F32),
                        pltpu.SemaphoreType.DMA((2,)), pltpu.SemaphoreType.DMA])
    return pl.pallas_call(
        _dispatch_kernel,
        out_shape=jax.ShapeDtypeStruct((n_rows, d), F32),
        grid_spec=grid_spec,
        compiler_params=pltpu.CompilerParams(dimension_semantics=("arbitrary",)),
        name="moe_dispatch",
    )(slots, ends, h2d, g.reshape(1, d))


def _moe_ffn_kernel(te_ref, nvt_ref, xs_ref, wg_hbm, wu_hbm, wd_hbm, y_ref,
                    wg, wu, wd, stage_in, stage_out, sem):
    i = pl.program_id(0)
    valid = i < nvt_ref[0]
    expert = te_ref[i]
    d, f = wg.shape
    n_slot = stage_in.shape[0]
    wc = stage_in.shape[2]
    n_chunk = f // wc

    @pl.when(jnp.logical_not(valid))
    def _():
        y_ref[...] = jnp.zeros_like(y_ref)

    def piece(p):
        mat, c = divmod(p, n_chunk)
        slot = p % n_slot
        if mat < 2:
            src = (wg_hbm, wu_hbm)[mat].at[expert, :, pl.ds(c * wc, wc)]
            stage = stage_in.at[slot]
            dst = (wg, wu)[mat].at[:, pl.ds(c * wc, wc)]
        else:
            src = wd_hbm.at[expert, pl.ds(c * wc, wc), :]
            stage = stage_out.at[slot]
            dst = wd.at[pl.ds(c * wc, wc), :]
        return pltpu.make_async_copy(src, stage, sem.at[slot]), stage, dst

    @pl.when(valid & ((i == 0) | (expert != te_ref[jnp.maximum(i - 1, 0)])))
    def _():
        n_piece = 3 * n_chunk
        for p in range(min(n_slot - 1, n_piece)):
            piece(p)[0].start()
        for p in range(n_piece):
            if p + n_slot - 1 < n_piece:
                piece(p + n_slot - 1)[0].start()
            copy, stage, dst = piece(p)
            copy.wait()
            dst[...] = stage[...].astype(BF16)

    @pl.when(valid)
    def _():
        x = xs_ref[...].astype(BF16)
        out = None
        for lo in range(0, f, TF_MOE):
            gte = _dot(x, wg[:, lo:lo + TF_MOE])
            up = _dot(x, wu[:, lo:lo + TF_MOE])
            act = (gte * jax.nn.sigmoid(gte) * up).astype(BF16)
            part = _dot(act, wd[lo:lo + TF_MOE, :])
            out = part if out is None else out + part
        y_ref[...] = out


def _moe_ffn(tile_expert, n_valid_tiles, xs, wg, wu, wd):
    d = xs.shape[1]
    n_exp, _, f = wg.shape
    ts = TS_MOE
    n_tiles = tile_expert.shape[0]
    hbm = pl.BlockSpec(memory_space=pl.ANY)
    grid_spec = pltpu.PrefetchScalarGridSpec(
        num_scalar_prefetch=2,
        grid=(n_tiles,),
        in_specs=[pl.BlockSpec((ts, d), lambda i, te, nvt: (i, 0)), hbm, hbm, hbm],
        out_specs=pl.BlockSpec((ts, d), lambda i, te, nvt: (i, 0)),
        scratch_shapes=[pltpu.VMEM((d, f), BF16), pltpu.VMEM((d, f), BF16), pltpu.VMEM((f, d), BF16),
                        pltpu.VMEM((W_STAGE_SLOTS, d, W_STAGE_COLS), F32),
                        pltpu.VMEM((W_STAGE_SLOTS, W_STAGE_COLS, d), F32),
                        pltpu.SemaphoreType.DMA((W_STAGE_SLOTS,))])
    return pl.pallas_call(
        _moe_ffn_kernel,
        out_shape=jax.ShapeDtypeStruct((n_tiles * ts, d), F32),
        grid_spec=grid_spec,
        compiler_params=pltpu.CompilerParams(dimension_semantics=("arbitrary",),
                                             vmem_limit_bytes=VMEM_LIMIT),
        name="moe_ffn",
    )(tile_expert, n_valid_tiles, xs, wg, wu, wd)


def _combine_kernel(slot_ref, h_ref, info_ref, y_hbm, o_ref, ybuf, sem):
    tm, d = h_ref.shape
    i = pl.program_id(0)
    n = pl.num_programs(0)

    def fetch(tile, start):
        buf = tile % 2

        def body(r, c):
            for k in range(2):
                cp = _row_copy(y_hbm, slot_ref[2 * (tile * tm + r) + k] if start else 0, ybuf.at[buf, k], r,
                               sem.at[buf])
                cp.start(priority=k) if start else cp.wait()
            return c
        lax.fori_loop(0, tm, body, 0, unroll=8)

    @pl.when(i == 0)
    def _():
        fetch(i, True)

    @pl.when(i + 1 < n)
    def _():
        fetch(i + 1, True)

    fetch(i, False)
    buf = i % 2
    w1 = info_ref[:, 4:5]
    w2 = info_ref[:, 5:6]
    o_ref[...] = h_ref[...] + (w1 * ybuf[buf, 0] + w2 * ybuf[buf, 1])


def _combine(slots, h2d, info, y):
    t, d = h2d.shape
    tm = TM_ROUTE
    grid_spec = pltpu.PrefetchScalarGridSpec(
        num_scalar_prefetch=1,
        grid=(t // tm,),
        in_specs=[pl.BlockSpec((tm, d), lambda i, sl: (i, 0)), pl.BlockSpec((tm, LANES), lambda i, sl: (i, 0)),
                  pl.BlockSpec(memory_space=pl.ANY)],
        out_specs=pl.BlockSpec((tm, d), lambda i, sl: (i, 0)),
        scratch_shapes=[pltpu.VMEM((2, 2, tm, d), F32), pltpu.SemaphoreType.DMA((2,))])
    return pl.pallas_call(
        _combine_kernel,
        out_shape=jax.ShapeDtypeStruct((t, d), F32),
        grid_spec=grid_spec,
        compiler_params=pltpu.CompilerParams(dimension_semantics=("arbitrary",)),
        name="moe_combine",
    )(slots, h2d, info, y)


def _moe(h2d, g, router, wg, wu, wd):
    t, d = h2d.shape
    ts = TS_MOE
    router_pad = jnp.pad(router, ((0, 0), (0, LANES - N_EXPERTS)))
    info, counts = _router(h2d, g, router_pad)

    counts = counts[0, :N_EXPERTS].astype(jnp.int32)
    tiles_per_expert = (counts + ts - 1) // ts
    tile_end = jnp.cumsum(tiles_per_expert)
    starts = (tile_end - tiles_per_expert) * ts
    n_valid = tile_end[-1]
    chosen = info[:, 0:2].astype(jnp.int32)[..., None] == jnp.arange(N_EXPERTS, dtype=jnp.int32)
    slots = (jnp.sum(jnp.where(chosen, starts, 0), axis=-1) + info[:, 2:4].astype(jnp.int32)).reshape(-1)
    n_tiles = 2 * t // ts + N_EXPERTS
    tile = jnp.minimum(jnp.arange(n_tiles, dtype=jnp.int32), n_valid - 1)
    tile_expert = jnp.sum((tile[:, None] >= tile_end[None, :]).astype(jnp.int32), axis=1)
    ends = jnp.concatenate([starts + counts, n_valid.reshape(1)])

    xs = _dispatch(slots, ends, h2d, g, (n_tiles + 1) * ts)
    y = _moe_ffn(tile_expert, n_valid.reshape(1), xs, wg, wu, wd)
    return _combine(slots, h2d, info, y)


def _regroup_heads(w, axis):
    shape = w.shape
    split = shape[:axis] + (SWA_KV_HEADS, SWA_GROUP, HEAD_DIM) + shape[axis + 1:]
    return jnp.swapaxes(w.reshape(split), axis, axis + 1).reshape(shape)


def kernel(x, mem, mem_norm_g, w_mem_kv, mem_k_norm_g, cv_attn_norm_g, cv_w_in, cv_b_glu, cv_dw_w, cv_dw_b, cv_ln_g, cv_ln_b, cv_memq_norm_g, cv_w_out, cv_ffn_norm_g, cv_w_gate, cv_w_up, cv_w_down, sw_attn_norm_g, sw_w_in, sw_q_norm_g, sw_k_norm_g, sw_sinks, sw_memq_norm_g, sw_w_out, sw_ffn_norm_g, sw_router, sw_we_gate, sw_we_up, sw_we_down):
    b, s, d = x.shape
    scale = HEAD_DIM ** -0.5
    lane_head = jnp.arange(MEM_WIDTH) // HEAD_DIM
    head_ones = (lane_head[:, None] == lane_head[None, :]).astype(BF16)
    tile4 = lambda gvec: jnp.tile(gvec, MEM_HEADS).reshape(1, MEM_WIDTH)

    kexp, vexp = _mem_kv(mem, mem_norm_g, w_mem_kv.astype(BF16), tile4(mem_k_norm_g), head_ones)

    h = x
    depth = cv_w_in.shape[0] + sw_w_in.shape[0]
    for i in range(depth):
        j = i // 2
        if i % 2 == 0:
            h = _conv_mixer(h, cv_attn_norm_g[j], cv_w_in[j].astype(BF16), cv_b_glu[j], cv_dw_w[j], cv_dw_b[j],
                            cv_ln_g[j], cv_ln_b[j], tile4(cv_memq_norm_g[j]) * scale, head_ones, kexp, vexp,
                            cv_w_out[j].astype(BF16))
            h = _dense_ffn(h.reshape(b * s, d), cv_ffn_norm_g[j], cv_w_gate[j].astype(BF16),
                           cv_w_up[j].astype(BF16), cv_w_down[j].astype(BF16)).reshape(b, s, d)
        else:
            qw = SWA_KV_HEADS * SWA_GROUP * HEAD_DIM
            w_in = jnp.concatenate([_regroup_heads(sw_w_in[j][:, :qw], 1), sw_w_in[j][:, qw:]], axis=1)
            w_out = jnp.concatenate([_regroup_heads(sw_w_out[j][:qw], 0), sw_w_out[j][qw:]], axis=0)
            h = _swa_mixer(h, sw_attn_norm_g[j], w_in.astype(BF16), tile4(sw_q_norm_g[j]) * scale,
                           tile4(sw_k_norm_g[j]), sw_sinks[j], tile4(sw_memq_norm_g[j]) * scale, head_ones,
                           kexp, vexp, w_out.astype(BF16))
            h = _moe(h.reshape(b * s, d), sw_ffn_norm_g[j], sw_router[j], sw_we_gate[j], sw_we_up[j],
                     sw_we_down[j]).reshape(b, s, d)
    return h
```

```python
import jax
import jax.numpy as jnp
from jax import lax
from jax.experimental import pallas as pl
from jax.experimental.pallas import tpu as pltpu

F32 = jnp.float32
BF16 = jnp.bfloat16

HEAD_DIM = 64
MEM_HEADS = 4
MEM_WIDTH = MEM_HEADS * HEAD_DIM
CONV_WIDTH = 31
CONV_HALO = 32
SWA_KV_HEADS = 4
SWA_GROUP = 3
SWA_BLOCK = 128
N_EXPERTS = 8
RMS_EPS = 1e-6
LN_EPS = 1e-5
ALIBI_MAX_BIAS = 8.0
NEG_INF = -1e30
LANES = 128
SUBLANES = 8
VMEM_LIMIT = 56 * 1024 * 1024

TM_MIX = 512
TM_FFN = 512
FFN_SPLIT = 1536
TM_ROUTE = 512
TS_MOE = 512
TF_MOE = 1792
W_STAGE_ROWS = 256
W_STAGE_COLS = 1024
W_STAGE_SLOTS = 6


def _dot(a, b):
    return jnp.dot(a, b, preferred_element_type=F32)


def _dot_nt(a, b):
    return lax.dot_general(a, b, (((1,), (1,)), ((), ())), preferred_element_type=F32)


def _rms_rows(x, g):
    ms = jnp.mean(x * x, axis=-1, keepdims=True)
    return x * lax.rsqrt(ms + RMS_EPS) * g


def _split_bf16(x):
    hi = x.astype(BF16)
    return hi, (x - hi.astype(F32)).astype(BF16)


def _head_mean_sq(x, head_ones):
    hi, lo = _split_bf16(x * x)
    return (_dot(hi, head_ones) + _dot(lo, head_ones)) * (1.0 / HEAD_DIM)


def _head_rms(x, head_ones, g):
    return x * lax.rsqrt(_head_mean_sq(x, head_ones) + RMS_EPS) * g


def _head_lane_mask(h):
    lane = lax.broadcasted_iota(jnp.int32, (1, MEM_WIDTH), 1)
    return ((lane >= h * HEAD_DIM) & (lane < (h + 1) * HEAD_DIM)).astype(F32)


def _mem_attention(q_mem, head_ones, qg_scaled, kexp, vexp):
    qn = _head_rms(q_mem, head_ones, qg_scaled).astype(BF16)
    s = _dot_nt(qn, kexp)
    m_len = kexp.shape[0] // MEM_HEADS
    ps = []
    for h in range(MEM_HEADS):
        sh = s[:, h * m_len:(h + 1) * m_len]
        e = jnp.exp(sh - jnp.max(sh, axis=-1, keepdims=True))
        ps.append((e / jnp.sum(e, axis=-1, keepdims=True)).astype(BF16))
    return _dot(jnp.concatenate(ps, axis=-1), vexp)


def _mem_kv_kernel(mem_ref, g_ref, w_ref, kg_ref, ones_ref, kexp_ref, vexp_ref):
    x = mem_ref[0]
    hn = _rms_rows(x, g_ref[...]).astype(BF16)
    kv = _dot(hn, w_ref[...])
    k = _head_rms(kv[:, :MEM_WIDTH], ones_ref[...], kg_ref[...])
    v = kv[:, MEM_WIDTH:]
    m_len = x.shape[0]
    for h in range(MEM_HEADS):
        mask = _head_lane_mask(h)
        kexp_ref[0, h * m_len:(h + 1) * m_len, :] = (k * mask).astype(BF16)
        vexp_ref[0, h * m_len:(h + 1) * m_len, :] = (v * mask).astype(BF16)


def _mem_kv(mem, g, w_bf, kg_tiled, head_ones):
    b, m_len, d = mem.shape
    out = jax.ShapeDtypeStruct((b, MEM_HEADS * m_len, MEM_WIDTH), BF16)
    const = lambda shape: pl.BlockSpec(shape, lambda i: (0,) * len(shape))
    return pl.pallas_call(
        _mem_kv_kernel,
        out_shape=(out, out),
        grid=(b,),
        in_specs=[pl.BlockSpec((1, m_len, d), lambda i: (i, 0, 0)),
                  const((1, d)), const(w_bf.shape), const((1, MEM_WIDTH)), const(head_ones.shape)],
        out_specs=(pl.BlockSpec((1, MEM_HEADS * m_len, MEM_WIDTH), lambda i: (i, 0, 0)),) * 2,
        name="mem_kv",
    )(mem, g.reshape(1, d), w_bf, kg_tiled, head_ones)


def _conv_mixer_kernel(x_ref, g_ref, win_ref, bglu_ref, dww_ref, dwb_ref, lng_ref, lnb_ref,
                       mqg_ref, ones_ref, kexp_ref, vexp_ref, wout_ref, o_ref, ubuf, cbuf):
    tm = x_ref.shape[1]
    cc = dww_ref.shape[1]

    @pl.when(pl.program_id(1) == 0)
    def _():
        ubuf[...] = jnp.zeros_like(ubuf)

    @pl.when(pl.program_id(1) > 0)
    def _():
        ubuf[:, 0:CONV_HALO, :] = ubuf[:, tm:tm + CONV_HALO, :]

    x = x_ref[0]
    hn = _rms_rows(x, g_ref[...]).astype(BF16)
    proj = _dot(hn, win_ref[...])
    a = proj[:, :cc] + bglu_ref[:, :cc]
    gate = proj[:, cc:2 * cc] + bglu_ref[:, cc:]
    u = a * jax.nn.sigmoid(gate)
    for r in range(SUBLANES):
        ubuf[r, CONV_HALO - r:CONV_HALO - r + tm, :] = u

    rows = 32
    base = CONV_HALO - (CONV_WIDTH - 1)
    for r0 in range(0, tm, rows):
        acc = jnp.broadcast_to(dwb_ref[...], (rows, cc))
        for k in range(CONV_WIDTH):
            r = (base + k) % SUBLANES
            lo = r0 + base + k - r
            acc = acc + ubuf[r, lo:lo + rows, :] * dww_ref[k:k + 1, :]
        mu = jnp.mean(acc, axis=-1, keepdims=True)
        dlt = acc - mu
        var = jnp.mean(dlt * dlt, axis=-1, keepdims=True)
        y = dlt * lax.rsqrt(var + LN_EPS) * lng_ref[...] + lnb_ref[...]
        cbuf[r0:r0 + rows, :] = (y * jax.nn.sigmoid(y)).astype(BF16)

    m = _mem_attention(proj[:, 2 * cc:], ones_ref[...], mqg_ref[...], kexp_ref[0], vexp_ref[0])
    out = _dot(cbuf[...], wout_ref[0:cc, :]) + _dot(m.astype(BF16), wout_ref[cc:, :])
    o_ref[0] = x + out


def _conv_mixer(h, g, win_bf, bglu, dww, dwb, lng, lnb, mqg_scaled, head_ones, kexp, vexp, wout_bf):
    b, s, d = h.shape
    cc = dww.shape[1]
    tm = TM_MIX
    const = lambda shape: pl.BlockSpec(shape, lambda i, j: (0,) * len(shape))
    per_batch = lambda arr: pl.BlockSpec((1,) + arr.shape[1:], lambda i, j: (i, 0, 0))
    return pl.pallas_call(
        _conv_mixer_kernel,
        out_shape=jax.ShapeDtypeStruct(h.shape, F32),
        grid=(b, s // tm),
        in_specs=[pl.BlockSpec((1, tm, d), lambda i, j: (i, j, 0)),
                  const((1, d)), const(win_bf.shape), const((1, 2 * cc)), const(dww.shape),
                  const((1, cc)), const((1, cc)), const((1, cc)), const((1, MEM_WIDTH)),
                  const(head_ones.shape), per_batch(kexp), per_batch(vexp), const(wout_bf.shape)],
        out_specs=pl.BlockSpec((1, tm, d), lambda i, j: (i, j, 0)),
        scratch_shapes=[pltpu.VMEM((SUBLANES, CONV_HALO + tm, cc), F32), pltpu.VMEM((tm, cc), BF16)],
        compiler_params=pltpu.CompilerParams(dimension_semantics=("arbitrary", "arbitrary"),
                                             vmem_limit_bytes=VMEM_LIMIT),
        name="conv_mixer",
    )(h, g.reshape(1, d), win_bf, bglu.reshape(1, -1), dww, dwb.reshape(1, cc), lng.reshape(1, cc),
      lnb.reshape(1, cc), mqg_scaled, head_ones, kexp, vexp, wout_bf)


def _dense_ffn_kernel(x_ref, g_ref, wg_ref, wu_ref, wd_ref, o_ref):
    x = x_ref[...]
    hn = _rms_rows(x, g_ref[...]).astype(BF16)
    f = wg_ref.shape[1]
    out = x
    for lo, hi in ((0, FFN_SPLIT), (FFN_SPLIT, f)):
        gte = _dot(hn, wg_ref[:, lo:hi])
        up = _dot(hn, wu_ref[:, lo:hi])
        act = (gte * jax.nn.sigmoid(gte) * up).astype(BF16)
        out = out + _dot(act, wd_ref[lo:hi, :])
    o_ref[...] = out


def _dense_ffn(h2d, g, wg_bf, wu_bf, wd_bf):
    t, d = h2d.shape
    tm = TM_FFN
    resident = lambda shape: pl.BlockSpec(shape, lambda i: (0, 0), pipeline_mode=pl.Buffered(1))
    return pl.pallas_call(
        _dense_ffn_kernel,
        out_shape=jax.ShapeDtypeStruct((t, d), F32),
        grid=(t // tm,),
        in_specs=[pl.BlockSpec((tm, d), lambda i: (i, 0)), resident((1, d)),
                  resident(wg_bf.shape), resident(wu_bf.shape), resident(wd_bf.shape)],
        out_specs=pl.BlockSpec((tm, d), lambda i: (i, 0)),
        compiler_params=pltpu.CompilerParams(dimension_semantics=("arbitrary",),
                                             vmem_limit_bytes=VMEM_LIMIT),
        name="dense_ffn",
    )(h2d, g.reshape(1, d), wg_bf, wu_bf, wd_bf)


def _alibi_slope(head):
    n_heads = SWA_KV_HEADS * SWA_GROUP
    return 2.0 ** (-ALIBI_MAX_BIAS * (head + 1.0) / n_heads)


def _swa_mixer_kernel(sinks_ref, x_ref, g_ref, win_ref, qg_ref, kg_ref, mqg_ref, ones_ref,
                      kexp_ref, vexp_ref, wout_ref, o_ref, kbuf, vbuf, obuf):
    tm = x_ref.shape[1]
    blk = SWA_BLOCK
    kvw = SWA_KV_HEADS * HEAD_DIM
    qw = SWA_GROUP * kvw
    first_tile = pl.program_id(1) == 0

    @pl.when(first_tile)
    def _():
        kbuf[:, 0:blk, :] = jnp.zeros((SWA_KV_HEADS, blk, kvw), BF16)
        vbuf[:, 0:blk, :] = jnp.zeros((SWA_KV_HEADS, blk, kvw), BF16)

    x = x_ref[0]
    hn = _rms_rows(x, g_ref[...]).astype(BF16)
    proj = _dot(hn, win_ref[...])
    ones = ones_ref[...]
    kn = _head_rms(proj[:, qw:qw + kvw], ones, kg_ref[...])
    v = proj[:, qw + kvw:qw + 2 * kvw]
    for h in range(SWA_KV_HEADS):
        mask = _head_lane_mask(h)
        kbuf[h, blk:blk + tm, :] = (kn * mask).astype(BF16)
        vbuf[h, blk:blk + tm, :] = (v * mask).astype(BF16)

    qi = lax.broadcasted_iota(jnp.int32, (blk, 2 * blk), 0)
    ci = lax.broadcasted_iota(jnp.int32, (blk, 2 * blk), 1)
    dist_i = qi + blk - ci
    dist = dist_i.astype(F32)
    win = (dist_i >= 0) & (dist_i < blk)
    win_first = win & ((ci >= blk) | jnp.logical_not(first_tile))

    for g in range(SWA_GROUP):
        qn = _head_rms(proj[:, g * kvw:(g + 1) * kvw], ones, qg_ref[...]).astype(BF16)
        for n in range(tm // blk):
            mask = win_first if n == 0 else win
            qb = qn[n * blk:(n + 1) * blk, :]
            og = jnp.zeros((blk, kvw), F32)
            for h in range(SWA_KV_HEADS):
                head = h * SWA_GROUP + g
                s = _dot_nt(qb, kbuf[h, n * blk:(n + 2) * blk, :])
                s = jnp.where(mask, s - _alibi_slope(head) * dist, NEG_INF)
                sink = sinks_ref[head]
                mx = jnp.maximum(jnp.max(s, axis=-1, keepdims=True), sink)
                e = jnp.exp(s - mx)
                den = jnp.sum(e, axis=-1, keepdims=True) + jnp.exp(sink - mx)
                og = og + _dot((e / den).astype(BF16), vbuf[h, n * blk:(n + 2) * blk, :])
            obuf[n * blk:(n + 1) * blk, g * kvw:(g + 1) * kvw] = og.astype(BF16)

    kbuf[:, 0:blk, :] = kbuf[:, tm:tm + blk, :]
    vbuf[:, 0:blk, :] = vbuf[:, tm:tm + blk, :]

    m = _mem_attention(proj[:, qw + 2 * kvw:], ones, mqg_ref[...], kexp_ref[0], vexp_ref[0])
    out = _dot(obuf[...], wout_ref[0:qw, :]) + _dot(m.astype(BF16), wout_ref[qw:, :])
    o_ref[0] = x + out


def _swa_mixer(h, g, win_bf, qg_scaled, kg_tiled, sinks, mqg_scaled, head_ones, kexp, vexp, wout_bf):
    b, s, d = h.shape
    tm = TM_MIX
    kvw = SWA_KV_HEADS * HEAD_DIM
    const = lambda shape: pl.BlockSpec(shape, lambda i, j, sk: (0,) * len(shape))
    per_batch = lambda arr: pl.BlockSpec((1,) + arr.shape[1:], lambda i, j, sk: (i, 0, 0))
    grid_spec = pltpu.PrefetchScalarGridSpec(
        num_scalar_prefetch=1,
        grid=(b, s // tm),
        in_specs=[pl.BlockSpec((1, tm, d), lambda i, j, sk: (i, j, 0)),
                  const((1, d)), const(win_bf.shape), const((1, kvw)), const((1, kvw)),
                  const((1, MEM_WIDTH)), const(head_ones.shape), per_batch(kexp), per_batch(vexp),
                  const(wout_bf.shape)],
        out_specs=pl.BlockSpec((1, tm, d), lambda i, j, sk: (i, j, 0)),
        scratch_shapes=[pltpu.VMEM((SWA_KV_HEADS, SWA_BLOCK + tm, kvw), BF16),
                        pltpu.VMEM((SWA_KV_HEADS, SWA_BLOCK + tm, kvw), BF16),
                        pltpu.VMEM((tm, SWA_GROUP * kvw), BF16)])
    return pl.pallas_call(
        _swa_mixer_kernel,
        out_shape=jax.ShapeDtypeStruct(h.shape, F32),
        grid_spec=grid_spec,
        compiler_params=pltpu.CompilerParams(dimension_semantics=("arbitrary", "arbitrary"),
                                             vmem_limit_bytes=VMEM_LIMIT),
        name="swa_mixer",
    )(sinks, h, g.reshape(1, d), win_bf, qg_scaled, kg_tiled, mqg_scaled, head_ones, kexp, vexp, wout_bf)


def _router_kernel(x_ref, g_ref, r_ref, info_ref, cnt_ref, run):
    tm = x_ref.shape[0]

    @pl.when(pl.program_id(0) == 0)
    def _():
        run[...] = jnp.zeros_like(run)

    hn_hi, hn_lo = _split_bf16(_rms_rows(x_ref[...], g_ref[...]))
    r_hi, r_lo = _split_bf16(r_ref[...])
    logits = _dot(hn_hi, r_hi) + (_dot(hn_hi, r_lo) + _dot(hn_lo, r_hi))
    lane = lax.broadcasted_iota(jnp.int32, (tm, LANES), 1).astype(F32)
    logits = jnp.where(lane < N_EXPERTS, logits, -jnp.inf)
    m1 = jnp.max(logits, axis=-1, keepdims=True)
    e1 = jnp.min(jnp.where(logits == m1, lane, float(LANES)), axis=-1, keepdims=True)
    oh1 = lane == e1
    rest = jnp.where(oh1, -jnp.inf, logits)
    m2 = jnp.max(rest, axis=-1, keepdims=True)
    e2 = jnp.min(jnp.where(rest == m2, lane, float(LANES)), axis=-1, keepdims=True)
    oh2 = lane == e2
    z = jnp.exp(m2 - m1)
    w1 = 1.0 / (1.0 + z)
    w2 = z / (1.0 + z)

    both = jnp.where(oh1 | oh2, 1.0, 0.0)
    ri = lax.broadcasted_iota(jnp.int32, (tm, tm), 0)
    ci = lax.broadcasted_iota(jnp.int32, (tm, tm), 1)
    before = jnp.where(ci < ri, 1.0, 0.0).astype(BF16)
    cnt = _dot(before, both.astype(BF16)) + run[...]
    r1 = jnp.sum(jnp.where(oh1, cnt, 0.0), axis=-1, keepdims=True)
    r2 = jnp.sum(jnp.where(oh2, cnt, 0.0), axis=-1, keepdims=True)
    run[...] = run[...] + jnp.sum(both, axis=0, keepdims=True)
    cnt_ref[...] = run[...]

    info = jnp.zeros((tm, LANES), F32)
    for idx, val in enumerate((e1, e2, r1, r2, w1, w2)):
        info = jnp.where(lane == idx, val, info)
    info_ref[...] = info


def _router(h2d, g, router_pad):
    t, d = h2d.shape
    tm = TM_ROUTE
    const = lambda shape: pl.BlockSpec(shape, lambda i: (0, 0))
    return pl.pallas_call(
        _router_kernel,
        out_shape=(jax.ShapeDtypeStruct((t, LANES), F32), jax.ShapeDtypeStruct((1, LANES), F32)),
        grid=(t // tm,),
        in_specs=[pl.BlockSpec((tm, d), lambda i: (i, 0)), const((1, d)), const(router_pad.shape)],
        out_specs=(pl.BlockSpec((tm, LANES), lambda i: (i, 0)), const((1, LANES))),
        scratch_shapes=[pltpu.VMEM((1, LANES), F32)],
        compiler_params=pltpu.CompilerParams(dimension_semantics=("arbitrary",)),
        name="router",
    )(h2d, g.reshape(1, d), router_pad)


def _row_copy(src, src_row, dst, dst_row, sem):
    s = pl.ds(pl.multiple_of(src_row * SUBLANES, SUBLANES), SUBLANES)
    d = pl.ds(pl.multiple_of(dst_row * SUBLANES, SUBLANES), SUBLANES)
    return pltpu.make_async_copy(src.at[s], dst.at[d], sem)


def _store_tile_rows(ref, x):
    n = x.shape[0]
    for c in range(SUBLANES):
        ref[pl.ds(c, n, stride=SUBLANES), :] = x[:, c * LANES:(c + 1) * LANES]


def _load_tile_rows(ref):
    n = ref.shape[0] // SUBLANES
    return jnp.concatenate([ref[pl.ds(c, n, stride=SUBLANES), :] for c in range(SUBLANES)], axis=-1)


def _dispatch_kernel(slot_ref, end_ref, x_ref, g_ref, xs_hbm, hbuf, zbuf, sem, zsem):
    tm = x_ref.shape[0]
    ts = zbuf.shape[0] // SUBLANES
    i = pl.program_id(0)

    @pl.when(i == 0)
    def _():
        zbuf[...] = jnp.zeros_like(zbuf)

        def zero_tile(first_slot):
            rows = pl.ds(pl.multiple_of(first_slot * SUBLANES, SUBLANES), ts * SUBLANES)
            block = pltpu.make_async_copy(zbuf, xs_hbm.at[rows], zsem)
            block.start()
            block.wait()

        for e in range(N_EXPERTS):
            zero_tile(end_ref[e])
        n_tiles = xs_hbm.shape[0] // (ts * SUBLANES)
        for q in range(n_tiles - slot_ref.shape[0] // ts):
            tile = end_ref[N_EXPERTS] + q

            @pl.when(tile < n_tiles)
            def _():
                zero_tile(tile * ts)

    def drain(buf):
        def wait(r, c):
            for k in range(2):
                _row_copy(hbuf.at[buf], r, xs_hbm, 0, sem.at[buf]).wait()
            return c
        lax.fori_loop(0, tm, wait, 0, unroll=8)

    buf = i % 2

    @pl.when(i >= 2)
    def _():
        drain(buf)

    _store_tile_rows(hbuf.at[buf], _rms_rows(x_ref[...], g_ref[...]))

    def issue(r, c):
        for k in range(2):
            _row_copy(hbuf.at[buf], r, xs_hbm, slot_ref[2 * (i * tm + r) + k], sem.at[buf]).start(priority=k)
        return c
    lax.fori_loop(0, tm, issue, 0, unroll=8)

    n_steps = slot_ref.shape[0] // (2 * tm)

    @pl.when(i == n_steps - 1)
    def _():
        if n_steps > 1:
            drain(1 - buf)
        drain(buf)


def _dispatch(slots, ends, h2d, g, n_slots):
    t, d = h2d.shape
    assert d == SUBLANES * LANES, "a slot row must fill exactly one (8, 128) tile"
    tm = TM_ROUTE
    grid_spec = pltpu.PrefetchScalarGridSpec(
        num_scalar_prefetch=2,
        grid=(t // tm,),
        in_specs=[pl.BlockSpec((tm, d), lambda i, sl, cn: (i, 0)), pl.BlockSpec((1, d), lambda i, sl, cn: (0, 0))],
        out_specs=pl.BlockSpec(memory_space=pl.ANY),
        scratch_shapes=[pltpu.VMEM((2, tm * SUBLANES, LANES), F32), pltpu.VMEM((TS_MOE * SUBLANES, LANES), F32),
                        pltpu.SemaphoreType.DMA((2,)), pltpu.SemaphoreType.DMA])
    return pl.pallas_call(
        _dispatch_kernel,
        out_shape=jax.ShapeDtypeStruct((n_slots * SUBLANES, LANES), F32),
        grid_spec=grid_spec,
        compiler_params=pltpu.CompilerParams(dimension_semantics=("arbitrary",)),
        name="moe_dispatch",
    )(slots, ends, h2d, g.reshape(1, d))


def _moe_ffn_kernel(te_ref, nvt_ref, xs_ref, wg_hbm, wu_hbm, wd_hbm, y_ref, wg, wu, wd, stage, sem):
    i = pl.program_id(0)
    valid = i < nvt_ref[0]
    expert = te_ref[i]
    f = wg.shape[1]
    n_slot, rows, cols = stage.shape

    @pl.when(jnp.logical_not(valid))
    def _():
        y_ref[...] = jnp.zeros_like(y_ref)

    pieces = [(src, dst, r0, c0, min(cols, dst.shape[1] - c0))
              for src, dst in ((wg_hbm, wg), (wu_hbm, wu), (wd_hbm, wd))
              for r0 in range(0, dst.shape[0], rows) for c0 in range(0, dst.shape[1], cols)]

    def copy(p):
        src, _, r0, c0, width = pieces[p]
        slot = p % n_slot
        return pltpu.make_async_copy(src.at[expert, pl.ds(r0, rows), pl.ds(c0, width)],
                                     stage.at[slot, :, pl.ds(0, width)], sem.at[slot])

    @pl.when(valid & ((i == 0) | (expert != te_ref[jnp.maximum(i - 1, 0)])))
    def _():
        for p in range(min(n_slot - 1, len(pieces))):
            copy(p).start()
        for p in range(len(pieces)):
            if p + n_slot - 1 < len(pieces):
                copy(p + n_slot - 1).start()
            copy(p).wait()
            _, dst, r0, c0, width = pieces[p]
            dst[r0:r0 + rows, c0:c0 + width] = stage[p % n_slot, :, 0:width].astype(BF16)

    @pl.when(valid)
    def _():
        x = _load_tile_rows(xs_ref).astype(BF16)
        out = None
        for lo in range(0, f, TF_MOE):
            gte = _dot(x, wg[:, lo:lo + TF_MOE])
            up = _dot(x, wu[:, lo:lo + TF_MOE])
            act = (gte * jax.nn.sigmoid(gte) * up).astype(BF16)
            part = _dot(act, wd[lo:lo + TF_MOE, :])
            out = part if out is None else out + part
        _store_tile_rows(y_ref, out)


def _moe_ffn(tile_expert, n_valid_tiles, xs, wg, wu, wd):
    n_exp, d, f = wg.shape
    ts = TS_MOE
    n_tiles = tile_expert.shape[0]
    hbm = pl.BlockSpec(memory_space=pl.ANY)
    tile_rows = pl.BlockSpec((ts * SUBLANES, LANES), lambda i, te, nvt: (i, 0))
    grid_spec = pltpu.PrefetchScalarGridSpec(
        num_scalar_prefetch=2,
        grid=(n_tiles,),
        in_specs=[tile_rows, hbm, hbm, hbm],
        out_specs=tile_rows,
        scratch_shapes=[pltpu.VMEM((d, f), BF16), pltpu.VMEM((d, f), BF16), pltpu.VMEM((f, d), BF16),
                        pltpu.VMEM((W_STAGE_SLOTS, W_STAGE_ROWS, W_STAGE_COLS), F32),
                        pltpu.SemaphoreType.DMA((W_STAGE_SLOTS,))])
    return pl.pallas_call(
        _moe_ffn_kernel,
        out_shape=jax.ShapeDtypeStruct((n_tiles * ts * SUBLANES, LANES), F32),
        grid_spec=grid_spec,
        compiler_params=pltpu.CompilerParams(dimension_semantics=("arbitrary",),
                                             vmem_limit_bytes=VMEM_LIMIT),
        name="moe_ffn",
    )(tile_expert, n_valid_tiles, xs, wg, wu, wd)


def _combine_kernel(slot_ref, h_ref, info_ref, y_hbm, o_ref, ybuf, sem):
    tm, d = h_ref.shape
    i = pl.program_id(0)
    n = pl.num_programs(0)

    def fetch(tile, start):
        buf = tile % 2

        def body(r, c):
            for k in range(2):
                cp = _row_copy(y_hbm, slot_ref[2 * (tile * tm + r) + k] if start else 0, ybuf.at[buf, k], r,
                               sem.at[buf])
                cp.start(priority=k) if start else cp.wait()
            return c
        lax.fori_loop(0, tm, body, 0, unroll=8)

    @pl.when(i == 0)
    def _():
        fetch(i, True)

    @pl.when(i + 1 < n)
    def _():
        fetch(i + 1, True)

    fetch(i, False)
    buf = i % 2
    w1 = info_ref[:, 4:5]
    w2 = info_ref[:, 5:6]
    o_ref[...] = h_ref[...] + (w1 * _load_tile_rows(ybuf.at[buf, 0]) + w2 * _load_tile_rows(ybuf.at[buf, 1]))


def _combine(slots, h2d, info, y):
    t, d = h2d.shape
    tm = TM_ROUTE
    grid_spec = pltpu.PrefetchScalarGridSpec(
        num_scalar_prefetch=1,
        grid=(t // tm,),
        in_specs=[pl.BlockSpec((tm, d), lambda i, sl: (i, 0)), pl.BlockSpec((tm, LANES), lambda i, sl: (i, 0)),
                  pl.BlockSpec(memory_space=pl.ANY)],
        out_specs=pl.BlockSpec((tm, d), lambda i, sl: (i, 0)),
        scratch_shapes=[pltpu.VMEM((2, 2, tm * SUBLANES, LANES), F32), pltpu.SemaphoreType.DMA((2,))])
    return pl.pallas_call(
        _combine_kernel,
        out_shape=jax.ShapeDtypeStruct((t, d), F32),
        grid_spec=grid_spec,
        compiler_params=pltpu.CompilerParams(dimension_semantics=("arbitrary",)),
        name="moe_combine",
    )(slots, h2d, info, y)


def _moe(h2d, g, router, wg, wu, wd):
    t, d = h2d.shape
    ts = TS_MOE
    router_pad = jnp.pad(router, ((0, 0), (0, LANES - N_EXPERTS)))
    info, counts = _router(h2d, g, router_pad)

    counts = counts[0, :N_EXPERTS].astype(jnp.int32)
    tiles_per_expert = (counts + ts - 1) // ts
    tile_end = jnp.cumsum(tiles_per_expert)
    starts = (tile_end - tiles_per_expert) * ts
    n_valid = tile_end[-1]
    chosen = info[:, 0:2].astype(jnp.int32)[..., None] == jnp.arange(N_EXPERTS, dtype=jnp.int32)
    slots = (jnp.sum(jnp.where(chosen, starts, 0), axis=-1) + info[:, 2:4].astype(jnp.int32)).reshape(-1)
    n_tiles = 2 * t // ts + N_EXPERTS
    tile = jnp.minimum(jnp.arange(n_tiles, dtype=jnp.int32), n_valid - 1)
    tile_expert = jnp.sum((tile[:, None] >= tile_end[None, :]).astype(jnp.int32), axis=1)
    ends = jnp.concatenate([starts + counts, n_valid.reshape(1)])

    xs = _dispatch(slots, ends, h2d, g, (n_tiles + 1) * ts)
    y = _moe_ffn(tile_expert, n_valid.reshape(1), xs, wg, wu, wd)
    return _combine(slots, h2d, info, y)


def _regroup_heads(w, axis):
    shape = w.shape
    split = shape[:axis] + (SWA_KV_HEADS, SWA_GROUP, HEAD_DIM) + shape[axis + 1:]
    return jnp.swapaxes(w.reshape(split), axis, axis + 1).reshape(shape)


def kernel(x, mem, mem_norm_g, w_mem_kv, mem_k_norm_g, cv_attn_norm_g, cv_w_in, cv_b_glu, cv_dw_w, cv_dw_b, cv_ln_g, cv_ln_b, cv_memq_norm_g, cv_w_out, cv_ffn_norm_g, cv_w_gate, cv_w_up, cv_w_down, sw_attn_norm_g, sw_w_in, sw_q_norm_g, sw_k_norm_g, sw_sinks, sw_memq_norm_g, sw_w_out, sw_ffn_norm_g, sw_router, sw_we_gate, sw_we_up, sw_we_down):
    b, s, d = x.shape
    scale = HEAD_DIM ** -0.5
    lane_head = jnp.arange(MEM_WIDTH) // HEAD_DIM
    head_ones = (lane_head[:, None] == lane_head[None, :]).astype(BF16)
    tile4 = lambda gvec: jnp.tile(gvec, MEM_HEADS).reshape(1, MEM_WIDTH)

    kexp, vexp = _mem_kv(mem, mem_norm_g, w_mem_kv.astype(BF16), tile4(mem_k_norm_g), head_ones)

    h = x
    depth = cv_w_in.shape[0] + sw_w_in.shape[0]
    for i in range(depth):
        j = i // 2
        if i % 2 == 0:
            h = _conv_mixer(h, cv_attn_norm_g[j], cv_w_in[j].astype(BF16), cv_b_glu[j], cv_dw_w[j], cv_dw_b[j],
                            cv_ln_g[j], cv_ln_b[j], tile4(cv_memq_norm_g[j]) * scale, head_ones, kexp, vexp,
                            cv_w_out[j].astype(BF16))
            h = _dense_ffn(h.reshape(b * s, d), cv_ffn_norm_g[j], cv_w_gate[j].astype(BF16),
                           cv_w_up[j].astype(BF16), cv_w_down[j].astype(BF16)).reshape(b, s, d)
        else:
            qw = SWA_KV_HEADS * SWA_GROUP * HEAD_DIM
            w_in = jnp.concatenate([_regroup_heads(sw_w_in[j][:, :qw], 1), sw_w_in[j][:, qw:]], axis=1)
            w_out = jnp.concatenate([_regroup_heads(sw_w_out[j][:qw], 0), sw_w_out[j][qw:]], axis=0)
            h = _swa_mixer(h, sw_attn_norm_g[j], w_in.astype(BF16), tile4(sw_q_norm_g[j]) * scale,
                           tile4(sw_k_norm_g[j]), sw_sinks[j], tile4(sw_memq_norm_g[j]) * scale, head_ones,
                           kexp, vexp, w_out.astype(BF16))
            h = _moe(h.reshape(b * s, d), sw_ffn_norm_g[j], sw_router[j], sw_we_gate[j], sw_we_up[j],
                     sw_we_down[j]).reshape(b, s, d)
    return h
```

```python
import jax
import jax.numpy as jnp
from jax import lax
from jax.experimental import pallas as pl
from jax.experimental.pallas import tpu as pltpu

F32 = jnp.float32
BF16 = jnp.bfloat16

HEAD_DIM = 64
MEM_HEADS = 4
MEM_WIDTH = MEM_HEADS * HEAD_DIM
CONV_WIDTH = 31
CONV_HALO = 32
SWA_KV_HEADS = 4
SWA_GROUP = 3
SWA_BLOCK = 128
N_EXPERTS = 8
RMS_EPS = 1e-6
LN_EPS = 1e-5
ALIBI_MAX_BIAS = 8.0
NEG_INF = -1e30
LANES = 128
SUBLANES = 8
VMEM_LIMIT = 56 * 1024 * 1024

TM_MIX = 512
TM_FFN = 512
FFN_SPLIT = 1536
TM_ROUTE = 512
TS_MOE = 512
TF_MOE = 1792
W_STAGE_ROWS = 256
W_STAGE_COLS = 1024
W_STAGE_SLOTS = 6


def _dot(a, b):
    return jnp.dot(a, b, preferred_element_type=F32)


def _dot_nt(a, b):
    return lax.dot_general(a, b, (((1,), (1,)), ((), ())), preferred_element_type=F32)


def _rms_rows(x, g):
    ms = jnp.mean(x * x, axis=-1, keepdims=True)
    return x * lax.rsqrt(ms + RMS_EPS) * g


def _split_bf16(x):
    hi = x.astype(BF16)
    return hi, (x - hi.astype(F32)).astype(BF16)


def _head_mean_sq(x, head_ones):
    hi, lo = _split_bf16(x * x)
    return (_dot(hi, head_ones) + _dot(lo, head_ones)) * (1.0 / HEAD_DIM)


def _head_rms(x, head_ones, g):
    return x * lax.rsqrt(_head_mean_sq(x, head_ones) + RMS_EPS) * g


def _head_lane_mask(h):
    lane = lax.broadcasted_iota(jnp.int32, (1, MEM_WIDTH), 1)
    return ((lane >= h * HEAD_DIM) & (lane < (h + 1) * HEAD_DIM)).astype(F32)


def _mem_attention(q_mem, head_ones, qg_scaled, kexp, vexp):
    qn = _head_rms(q_mem, head_ones, qg_scaled).astype(BF16)
    s = _dot_nt(qn, kexp)
    m_len = kexp.shape[0] // MEM_HEADS
    ps = []
    for h in range(MEM_HEADS):
        sh = s[:, h * m_len:(h + 1) * m_len]
        e = jnp.exp(sh - jnp.max(sh, axis=-1, keepdims=True))
        ps.append((e * (1.0 / jnp.sum(e, axis=-1, keepdims=True))).astype(BF16))
    return _dot(jnp.concatenate(ps, axis=-1), vexp)


def _mem_kv_kernel(mem_ref, g_ref, w_ref, kg_ref, ones_ref, kexp_ref, vexp_ref):
    x = mem_ref[0]
    hn = _rms_rows(x, g_ref[...]).astype(BF16)
    kv = _dot(hn, w_ref[...])
    k = _head_rms(kv[:, :MEM_WIDTH], ones_ref[...], kg_ref[...])
    v = kv[:, MEM_WIDTH:]
    m_len = x.shape[0]
    for h in range(MEM_HEADS):
        mask = _head_lane_mask(h)
        kexp_ref[0, h * m_len:(h + 1) * m_len, :] = (k * mask).astype(BF16)
        vexp_ref[0, h * m_len:(h + 1) * m_len, :] = (v * mask).astype(BF16)


def _mem_kv(mem, g, w_bf, kg_tiled, head_ones):
    b, m_len, d = mem.shape
    out = jax.ShapeDtypeStruct((b, MEM_HEADS * m_len, MEM_WIDTH), BF16)
    const = lambda shape: pl.BlockSpec(shape, lambda i: (0,) * len(shape))
    return pl.pallas_call(
        _mem_kv_kernel,
        out_shape=(out, out),
        grid=(b,),
        in_specs=[pl.BlockSpec((1, m_len, d), lambda i: (i, 0, 0)),
                  const((1, d)), const(w_bf.shape), const((1, MEM_WIDTH)), const(head_ones.shape)],
        out_specs=(pl.BlockSpec((1, MEM_HEADS * m_len, MEM_WIDTH), lambda i: (i, 0, 0)),) * 2,
        name="mem_kv",
    )(mem, g.reshape(1, d), w_bf, kg_tiled, head_ones)


def _conv_mixer_kernel(x_ref, g_ref, win_ref, bglu_ref, dww_ref, dwb_ref, lng_ref, lnb_ref,
                       mqg_ref, ones_ref, kexp_ref, vexp_ref, wout_ref, o_ref, ubuf, cbuf):
    tm = x_ref.shape[1]
    cc = dww_ref.shape[1]

    @pl.when(pl.program_id(1) == 0)
    def _():
        ubuf[...] = jnp.zeros_like(ubuf)

    @pl.when(pl.program_id(1) > 0)
    def _():
        ubuf[:, 0:CONV_HALO, :] = ubuf[:, tm:tm + CONV_HALO, :]

    x = x_ref[0]
    hn = _rms_rows(x, g_ref[...]).astype(BF16)
    proj = _dot(hn, win_ref[...])
    a = proj[:, :cc] + bglu_ref[:, :cc]
    gate = proj[:, cc:2 * cc] + bglu_ref[:, cc:]
    u = a * jax.nn.sigmoid(gate)
    for r in range(SUBLANES):
        ubuf[r, CONV_HALO - r:CONV_HALO - r + tm, :] = u

    rows = 32
    base = CONV_HALO - (CONV_WIDTH - 1)
    for r0 in range(0, tm, rows):
        acc = jnp.broadcast_to(dwb_ref[...], (rows, cc))
        for k in range(CONV_WIDTH):
            r = (base + k) % SUBLANES
            lo = r0 + base + k - r
            acc = acc + ubuf[r, lo:lo + rows, :] * dww_ref[k:k + 1, :]
        mu = jnp.mean(acc, axis=-1, keepdims=True)
        dlt = acc - mu
        var = jnp.mean(dlt * dlt, axis=-1, keepdims=True)
        y = dlt * lax.rsqrt(var + LN_EPS) * lng_ref[...] + lnb_ref[...]
        cbuf[r0:r0 + rows, :] = (y * jax.nn.sigmoid(y)).astype(BF16)

    m = _mem_attention(proj[:, 2 * cc:], ones_ref[...], mqg_ref[...], kexp_ref[0], vexp_ref[0])
    out = _dot(cbuf[...], wout_ref[0:cc, :]) + _dot(m.astype(BF16), wout_ref[cc:, :])
    o_ref[0] = x + out


def _conv_mixer(h, g, win_bf, bglu, dww, dwb, lng, lnb, mqg_scaled, head_ones, kexp, vexp, wout_bf):
    b, s, d = h.shape
    cc = dww.shape[1]
    tm = TM_MIX
    const = lambda shape: pl.BlockSpec(shape, lambda i, j: (0,) * len(shape))
    per_batch = lambda arr: pl.BlockSpec((1,) + arr.shape[1:], lambda i, j: (i, 0, 0))
    return pl.pallas_call(
        _conv_mixer_kernel,
        out_shape=jax.ShapeDtypeStruct(h.shape, F32),
        grid=(b, s // tm),
        in_specs=[pl.BlockSpec((1, tm, d), lambda i, j: (i, j, 0)),
                  const((1, d)), const(win_bf.shape), const((1, 2 * cc)), const(dww.shape),
                  const((1, cc)), const((1, cc)), const((1, cc)), const((1, MEM_WIDTH)),
                  const(head_ones.shape), per_batch(kexp), per_batch(vexp), const(wout_bf.shape)],
        out_specs=pl.BlockSpec((1, tm, d), lambda i, j: (i, j, 0)),
        scratch_shapes=[pltpu.VMEM((SUBLANES, CONV_HALO + tm, cc), F32), pltpu.VMEM((tm, cc), BF16)],
        compiler_params=pltpu.CompilerParams(dimension_semantics=("arbitrary", "arbitrary"),
                                             vmem_limit_bytes=VMEM_LIMIT),
        name="conv_mixer",
    )(h, g.reshape(1, d), win_bf, bglu.reshape(1, -1), dww, dwb.reshape(1, cc), lng.reshape(1, cc),
      lnb.reshape(1, cc), mqg_scaled, head_ones, kexp, vexp, wout_bf)


def _dense_ffn_kernel(x_ref, g_ref, wg_ref, wu_ref, wd_ref, o_ref):
    x = x_ref[...]
    hn = _rms_rows(x, g_ref[...]).astype(BF16)
    f = wg_ref.shape[1]
    out = x
    for lo, hi in ((0, FFN_SPLIT), (FFN_SPLIT, f)):
        gte = _dot(hn, wg_ref[:, lo:hi])
        up = _dot(hn, wu_ref[:, lo:hi])
        act = (gte * jax.nn.sigmoid(gte) * up).astype(BF16)
        out = out + _dot(act, wd_ref[lo:hi, :])
    o_ref[...] = out


def _dense_ffn(h2d, g, wg_bf, wu_bf, wd_bf):
    t, d = h2d.shape
    tm = TM_FFN
    resident = lambda shape: pl.BlockSpec(shape, lambda i: (0, 0), pipeline_mode=pl.Buffered(1))
    return pl.pallas_call(
        _dense_ffn_kernel,
        out_shape=jax.ShapeDtypeStruct((t, d), F32),
        grid=(t // tm,),
        in_specs=[pl.BlockSpec((tm, d), lambda i: (i, 0)), resident((1, d)),
                  resident(wg_bf.shape), resident(wu_bf.shape), resident(wd_bf.shape)],
        out_specs=pl.BlockSpec((tm, d), lambda i: (i, 0)),
        compiler_params=pltpu.CompilerParams(dimension_semantics=("arbitrary",),
                                             vmem_limit_bytes=VMEM_LIMIT),
        name="dense_ffn",
    )(h2d, g.reshape(1, d), wg_bf, wu_bf, wd_bf)


def _swa_bias_tables():
    blk = SWA_BLOCK
    n_heads = SWA_KV_HEADS * SWA_GROUP
    slopes = jnp.exp2(-ALIBI_MAX_BIAS * (jnp.arange(n_heads, dtype=F32) + 1.0) / n_heads)
    dist = (jnp.arange(blk)[:, None] + blk - jnp.arange(2 * blk)[None, :]).astype(F32)
    window = (dist >= 0) & (dist < blk)
    bias = -slopes[:, None, None] * dist[None]
    current = jnp.arange(2 * blk)[None, :] >= blk
    return jnp.stack([jnp.where(window[None], bias, NEG_INF),
                      jnp.where((window & current)[None], bias, NEG_INF)])


def _swa_mixer_kernel(sinks_ref, x_ref, g_ref, win_ref, qg_ref, kg_ref, mqg_ref, ones_ref, bias_ref,
                      kexp_ref, vexp_ref, wout_ref, o_ref, kbuf, vbuf, obuf):
    tm = x_ref.shape[1]
    blk = SWA_BLOCK
    kvw = SWA_KV_HEADS * HEAD_DIM
    qw = SWA_GROUP * kvw
    first_tile = pl.program_id(1) == 0

    @pl.when(first_tile)
    def _():
        kbuf[:, 0:blk, :] = jnp.zeros((SWA_KV_HEADS, blk, kvw), BF16)
        vbuf[:, 0:blk, :] = jnp.zeros((SWA_KV_HEADS, blk, kvw), BF16)

    x = x_ref[0]
    hn = _rms_rows(x, g_ref[...]).astype(BF16)
    proj = _dot(hn, win_ref[...])
    ones = ones_ref[...]
    kn = _head_rms(proj[:, qw:qw + kvw], ones, kg_ref[...])
    v = proj[:, qw + kvw:qw + 2 * kvw]
    for h in range(SWA_KV_HEADS):
        mask = _head_lane_mask(h)
        kbuf[h, blk:blk + tm, :] = (kn * mask).astype(BF16)
        vbuf[h, blk:blk + tm, :] = (v * mask).astype(BF16)

    qn = [_head_rms(proj[:, g * kvw:(g + 1) * kvw], ones, qg_ref[...]).astype(BF16) for g in range(SWA_GROUP)]
    first_block = jnp.where(first_tile, 1, 0)
    for n in range(tm // blk):
        rows = slice(n * blk, (n + 1) * blk)
        keys = slice(n * blk, (n + 2) * blk)
        table = first_block if n == 0 else 0
        qs = jnp.concatenate([q[rows, :] for q in qn], axis=0)
        og = None
        for h in range(SWA_KV_HEADS):
            s = _dot_nt(qs, kbuf[h, keys, :])
            ps = []
            for g in range(SWA_GROUP):
                head = h * SWA_GROUP + g
                sg = s[g * blk:(g + 1) * blk, :] + bias_ref[table, head]
                sink = sinks_ref[head]
                mx = jnp.maximum(jnp.max(sg, axis=-1, keepdims=True), sink)
                e = jnp.exp(sg - mx)
                den = jnp.sum(e, axis=-1, keepdims=True) + jnp.exp(sink - mx)
                ps.append((e * (1.0 / den)).astype(BF16))
            part = _dot(jnp.concatenate(ps, axis=0), vbuf[h, keys, :])
            og = part if og is None else og + part
        for g in range(SWA_GROUP):
            obuf[rows, g * kvw:(g + 1) * kvw] = og[g * blk:(g + 1) * blk, :].astype(BF16)

    kbuf[:, 0:blk, :] = kbuf[:, tm:tm + blk, :]
    vbuf[:, 0:blk, :] = vbuf[:, tm:tm + blk, :]

    m = _mem_attention(proj[:, qw + 2 * kvw:], ones, mqg_ref[...], kexp_ref[0], vexp_ref[0])
    out = _dot(obuf[...], wout_ref[0:qw, :]) + _dot(m.astype(BF16), wout_ref[qw:, :])
    o_ref[0] = x + out


def _swa_mixer(h, g, win_bf, qg_scaled, kg_tiled, sinks, mqg_scaled, head_ones, kexp, vexp, wout_bf):
    b, s, d = h.shape
    tm = TM_MIX
    kvw = SWA_KV_HEADS * HEAD_DIM
    bias = _swa_bias_tables()
    const = lambda shape: pl.BlockSpec(shape, lambda i, j, sk: (0,) * len(shape))
    per_batch = lambda arr: pl.BlockSpec((1,) + arr.shape[1:], lambda i, j, sk: (i, 0, 0))
    grid_spec = pltpu.PrefetchScalarGridSpec(
        num_scalar_prefetch=1,
        grid=(b, s // tm),
        in_specs=[pl.BlockSpec((1, tm, d), lambda i, j, sk: (i, j, 0)),
                  const((1, d)), const(win_bf.shape), const((1, kvw)), const((1, kvw)),
                  const((1, MEM_WIDTH)), const(head_ones.shape), const(bias.shape), per_batch(kexp),
                  per_batch(vexp), const(wout_bf.shape)],
        out_specs=pl.BlockSpec((1, tm, d), lambda i, j, sk: (i, j, 0)),
        scratch_shapes=[pltpu.VMEM((SWA_KV_HEADS, SWA_BLOCK + tm, kvw), BF16),
                        pltpu.VMEM((SWA_KV_HEADS, SWA_BLOCK + tm, kvw), BF16),
                        pltpu.VMEM((tm, SWA_GROUP * kvw), BF16)])
    return pl.pallas_call(
        _swa_mixer_kernel,
        out_shape=jax.ShapeDtypeStruct(h.shape, F32),
        grid_spec=grid_spec,
        compiler_params=pltpu.CompilerParams(dimension_semantics=("arbitrary", "arbitrary"),
                                             vmem_limit_bytes=VMEM_LIMIT),
        name="swa_mixer",
    )(sinks, h, g.reshape(1, d), win_bf, qg_scaled, kg_tiled, mqg_scaled, head_ones, bias, kexp, vexp, wout_bf)


def _router_kernel(x_ref, g_ref, r_ref, info_ref, cnt_ref, run):
    tm = x_ref.shape[0]

    @pl.when(pl.program_id(0) == 0)
    def _():
        run[...] = jnp.zeros_like(run)

    hn_hi, hn_lo = _split_bf16(_rms_rows(x_ref[...], g_ref[...]))
    r_hi, r_lo = _split_bf16(r_ref[...])
    logits = _dot(hn_hi, r_hi) + (_dot(hn_hi, r_lo) + _dot(hn_lo, r_hi))
    lane = lax.broadcasted_iota(jnp.int32, (tm, LANES), 1).astype(F32)
    logits = jnp.where(lane < N_EXPERTS, logits, -jnp.inf)
    m1 = jnp.max(logits, axis=-1, keepdims=True)
    e1 = jnp.min(jnp.where(logits == m1, lane, float(LANES)), axis=-1, keepdims=True)
    oh1 = lane == e1
    rest = jnp.where(oh1, -jnp.inf, logits)
    m2 = jnp.max(rest, axis=-1, keepdims=True)
    e2 = jnp.min(jnp.where(rest == m2, lane, float(LANES)), axis=-1, keepdims=True)
    oh2 = lane == e2
    z = jnp.exp(m2 - m1)
    w1 = 1.0 / (1.0 + z)
    w2 = z / (1.0 + z)

    both = jnp.where(oh1 | oh2, 1.0, 0.0)
    ri = lax.broadcasted_iota(jnp.int32, (tm, tm), 0)
    ci = lax.broadcasted_iota(jnp.int32, (tm, tm), 1)
    before = jnp.where(ci < ri, 1.0, 0.0).astype(BF16)
    cnt = _dot(before, both.astype(BF16)) + run[...]
    r1 = jnp.sum(jnp.where(oh1, cnt, 0.0), axis=-1, keepdims=True)
    r2 = jnp.sum(jnp.where(oh2, cnt, 0.0), axis=-1, keepdims=True)
    run[...] = run[...] + jnp.sum(both, axis=0, keepdims=True)
    cnt_ref[...] = run[...]

    info = jnp.zeros((tm, LANES), F32)
    for idx, val in enumerate((e1, e2, r1, r2, w1, w2)):
        info = jnp.where(lane == idx, val, info)
    info_ref[...] = info


def _router(h2d, g, router_pad):
    t, d = h2d.shape
    tm = TM_ROUTE
    const = lambda shape: pl.BlockSpec(shape, lambda i: (0, 0))
    return pl.pallas_call(
        _router_kernel,
        out_shape=(jax.ShapeDtypeStruct((t, LANES), F32), jax.ShapeDtypeStruct((1, LANES), F32)),
        grid=(t // tm,),
        in_specs=[pl.BlockSpec((tm, d), lambda i: (i, 0)), const((1, d)), const(router_pad.shape)],
        out_specs=(pl.BlockSpec((tm, LANES), lambda i: (i, 0)), const((1, LANES))),
        scratch_shapes=[pltpu.VMEM((1, LANES), F32)],
        compiler_params=pltpu.CompilerParams(dimension_semantics=("arbitrary",)),
        name="router",
    )(h2d, g.reshape(1, d), router_pad)


def _row_copy(src, src_row, dst, dst_row, sem):
    s = pl.ds(pl.multiple_of(src_row * SUBLANES, SUBLANES), SUBLANES)
    d = pl.ds(pl.multiple_of(dst_row * SUBLANES, SUBLANES), SUBLANES)
    return pltpu.make_async_copy(src.at[s], dst.at[d], sem)


def _store_tile_rows(ref, x):
    n = x.shape[0]
    for c in range(SUBLANES):
        ref[pl.ds(c, n, stride=SUBLANES), :] = x[:, c * LANES:(c + 1) * LANES]


def _load_tile_rows(ref):
    n = ref.shape[0] // SUBLANES
    return jnp.concatenate([ref[pl.ds(c, n, stride=SUBLANES), :] for c in range(SUBLANES)], axis=-1)


def _dispatch_kernel(slot_ref, end_ref, x_ref, g_ref, xs_hbm, hbuf, zbuf, sem, zsem):
    tm = x_ref.shape[0]
    ts = zbuf.shape[0] // SUBLANES
    i = pl.program_id(0)

    @pl.when(i == 0)
    def _():
        zbuf[...] = jnp.zeros_like(zbuf)

        def zero_tile(first_slot):
            rows = pl.ds(pl.multiple_of(first_slot * SUBLANES, SUBLANES), ts * SUBLANES)
            block = pltpu.make_async_copy(zbuf, xs_hbm.at[rows], zsem)
            block.start()
            block.wait()

        for e in range(N_EXPERTS):
            zero_tile(end_ref[e])
        n_tiles = xs_hbm.shape[0] // (ts * SUBLANES)
        for q in range(n_tiles - slot_ref.shape[0] // ts):
            tile = end_ref[N_EXPERTS] + q

            @pl.when(tile < n_tiles)
            def _():
                zero_tile(tile * ts)

    def drain(buf):
        def wait(r, c):
            for k in range(2):
                _row_copy(hbuf.at[buf], r, xs_hbm, 0, sem.at[buf]).wait()
            return c
        lax.fori_loop(0, tm, wait, 0, unroll=8)

    buf = i % 2

    @pl.when(i >= 2)
    def _():
        drain(buf)

    _store_tile_rows(hbuf.at[buf], _rms_rows(x_ref[...], g_ref[...]))

    def issue(r, c):
        for k in range(2):
            _row_copy(hbuf.at[buf], r, xs_hbm, slot_ref[2 * (i * tm + r) + k], sem.at[buf]).start(priority=k)
        return c
    lax.fori_loop(0, tm, issue, 0, unroll=8)

    n_steps = slot_ref.shape[0] // (2 * tm)

    @pl.when(i == n_steps - 1)
    def _():
        if n_steps > 1:
            drain(1 - buf)
        drain(buf)


def _dispatch(slots, ends, h2d, g, n_slots):
    t, d = h2d.shape
    assert d == SUBLANES * LANES, "a slot row must fill exactly one (8, 128) tile"
    tm = TM_ROUTE
    grid_spec = pltpu.PrefetchScalarGridSpec(
        num_scalar_prefetch=2,
        grid=(t // tm,),
        in_specs=[pl.BlockSpec((tm, d), lambda i, sl, cn: (i, 0)), pl.BlockSpec((1, d), lambda i, sl, cn: (0, 0))],
        out_specs=pl.BlockSpec(memory_space=pl.ANY),
        scratch_shapes=[pltpu.VMEM((2, tm * SUBLANES, LANES), F32), pltpu.VMEM((TS_MOE * SUBLANES, LANES), F32),
                        pltpu.SemaphoreType.DMA((2,)), pltpu.SemaphoreType.DMA])
    return pl.pallas_call(
        _dispatch_kernel,
        out_shape=jax.ShapeDtypeStruct((n_slots * SUBLANES, LANES), F32),
        grid_spec=grid_spec,
        compiler_params=pltpu.CompilerParams(dimension_semantics=("arbitrary",)),
        name="moe_dispatch",
    )(slots, ends, h2d, g.reshape(1, d))


def _moe_ffn_kernel(te_ref, nvt_ref, xs_ref, wg_hbm, wu_hbm, wd_hbm, y_ref, wg, wu, wd, stage, sem):
    i = pl.program_id(0)
    valid = i < nvt_ref[0]
    expert = te_ref[i]
    f = wg.shape[1]
    n_slot, rows, cols = stage.shape

    @pl.when(jnp.logical_not(valid))
    def _():
        y_ref[...] = jnp.zeros_like(y_ref)

    pieces = [(src, dst, r0, c0, min(cols, dst.shape[1] - c0))
              for src, dst in ((wg_hbm, wg), (wu_hbm, wu), (wd_hbm, wd))
              for r0 in range(0, dst.shape[0], rows) for c0 in range(0, dst.shape[1], cols)]

    def copy(p):
        src, _, r0, c0, width = pieces[p]
        slot = p % n_slot
        return pltpu.make_async_copy(src.at[expert, pl.ds(r0, rows), pl.ds(c0, width)],
                                     stage.at[slot, :, pl.ds(0, width)], sem.at[slot])

    @pl.when(valid & ((i == 0) | (expert != te_ref[jnp.maximum(i - 1, 0)])))
    def _():
        for p in range(min(n_slot - 1, len(pieces))):
            copy(p).start()
        for p in range(len(pieces)):
            if p + n_slot - 1 < len(pieces):
                copy(p + n_slot - 1).start()
            copy(p).wait()
            _, dst, r0, c0, width = pieces[p]
            dst[r0:r0 + rows, c0:c0 + width] = stage[p % n_slot, :, 0:width].astype(BF16)

    @pl.when(valid)
    def _():
        x = _load_tile_rows(xs_ref).astype(BF16)
        out = None
        for lo in range(0, f, TF_MOE):
            gte = _dot(x, wg[:, lo:lo + TF_MOE])
            up = _dot(x, wu[:, lo:lo + TF_MOE])
            act = (gte * jax.nn.sigmoid(gte) * up).astype(BF16)
            part = _dot(act, wd[lo:lo + TF_MOE, :])
            out = part if out is None else out + part
        _store_tile_rows(y_ref, out)


def _moe_ffn(tile_expert, n_valid_tiles, xs, wg, wu, wd):
    n_exp, d, f = wg.shape
    ts = TS_MOE
    n_tiles = tile_expert.shape[0]
    hbm = pl.BlockSpec(memory_space=pl.ANY)
    tile_rows = pl.BlockSpec((ts * SUBLANES, LANES), lambda i, te, nvt: (i, 0))
    grid_spec = pltpu.PrefetchScalarGridSpec(
        num_scalar_prefetch=2,
        grid=(n_tiles,),
        in_specs=[tile_rows, hbm, hbm, hbm],
        out_specs=tile_rows,
        scratch_shapes=[pltpu.VMEM((d, f), BF16), pltpu.VMEM((d, f), BF16), pltpu.VMEM((f, d), BF16),
                        pltpu.VMEM((W_STAGE_SLOTS, W_STAGE_ROWS, W_STAGE_COLS), F32),
                        pltpu.SemaphoreType.DMA((W_STAGE_SLOTS,))])
    return pl.pallas_call(
        _moe_ffn_kernel,
        out_shape=jax.ShapeDtypeStruct((n_tiles * ts * SUBLANES, LANES), F32),
        grid_spec=grid_spec,
        compiler_params=pltpu.CompilerParams(dimension_semantics=("arbitrary",),
                                             vmem_limit_bytes=VMEM_LIMIT),
        name="moe_ffn",
    )(tile_expert, n_valid_tiles, xs, wg, wu, wd)


def _combine_kernel(slot_ref, h_ref, info_ref, y_hbm, o_ref, ybuf, sem):
    tm, d = h_ref.shape
    i = pl.program_id(0)
    n = pl.num_programs(0)

    def fetch(tile, start):
        buf = tile % 2

        def body(r, c):
            for k in range(2):
                cp = _row_copy(y_hbm, slot_ref[2 * (tile * tm + r) + k] if start else 0, ybuf.at[buf, k], r,
                               sem.at[buf])
                cp.start(priority=k) if start else cp.wait()
            return c
        lax.fori_loop(0, tm, body, 0, unroll=8)

    @pl.when(i == 0)
    def _():
        fetch(i, True)

    @pl.when(i + 1 < n)
    def _():
        fetch(i + 1, True)

    fetch(i, False)
    buf = i % 2
    w1 = info_ref[:, 4:5]
    w2 = info_ref[:, 5:6]
    o_ref[...] = h_ref[...] + (w1 * _load_tile_rows(ybuf.at[buf, 0]) + w2 * _load_tile_rows(ybuf.at[buf, 1]))


def _combine(slots, h2d, info, y):
    t, d = h2d.shape
    tm = TM_ROUTE
    grid_spec = pltpu.PrefetchScalarGridSpec(
        num_scalar_prefetch=1,
        grid=(t // tm,),
        in_specs=[pl.BlockSpec((tm, d), lambda i, sl: (i, 0)), pl.BlockSpec((tm, LANES), lambda i, sl: (i, 0)),
                  pl.BlockSpec(memory_space=pl.ANY)],
        out_specs=pl.BlockSpec((tm, d), lambda i, sl: (i, 0)),
        scratch_shapes=[pltpu.VMEM((2, 2, tm * SUBLANES, LANES), F32), pltpu.SemaphoreType.DMA((2,))])
    return pl.pallas_call(
        _combine_kernel,
        out_shape=jax.ShapeDtypeStruct((t, d), F32),
        grid_spec=grid_spec,
        compiler_params=pltpu.CompilerParams(dimension_semantics=("arbitrary",)),
        name="moe_combine",
    )(slots, h2d, info, y)


def _moe(h2d, g, router, wg, wu, wd):
    t, d = h2d.shape
    ts = TS_MOE
    router_pad = jnp.pad(router, ((0, 0), (0, LANES - N_EXPERTS)))
    info, counts = _router(h2d, g, router_pad)

    counts = counts[0, :N_EXPERTS].astype(jnp.int32)
    tiles_per_expert = (counts + ts - 1) // ts
    tile_end = jnp.cumsum(tiles_per_expert)
    starts = (tile_end - tiles_per_expert) * ts
    n_valid = tile_end[-1]
    chosen = info[:, 0:2].astype(jnp.int32)[..., None] == jnp.arange(N_EXPERTS, dtype=jnp.int32)
    slots = (jnp.sum(jnp.where(chosen, starts, 0), axis=-1) + info[:, 2:4].astype(jnp.int32)).reshape(-1)
    n_tiles = 2 * t // ts + N_EXPERTS
    tile = jnp.minimum(jnp.arange(n_tiles, dtype=jnp.int32), n_valid - 1)
    tile_expert = jnp.sum((tile[:, None] >= tile_end[None, :]).astype(jnp.int32), axis=1)
    ends = jnp.concatenate([starts + counts, n_valid.reshape(1)])

    xs = _dispatch(slots, ends, h2d, g, (n_tiles + 1) * ts)
    y = _moe_ffn(tile_expert, n_valid.reshape(1), xs, wg, wu, wd)
    return _combine(slots, h2d, info, y)


def _regroup_heads(w, axis):
    shape = w.shape
    split = shape[:axis] + (SWA_KV_HEADS, SWA_GROUP, HEAD_DIM) + shape[axis + 1:]
    return jnp.swapaxes(w.reshape(split), axis, axis + 1).reshape(shape)


def kernel(x, mem, mem_norm_g, w_mem_kv, mem_k_norm_g, cv_attn_norm_g, cv_w_in, cv_b_glu, cv_dw_w, cv_dw_b, cv_ln_g, cv_ln_b, cv_memq_norm_g, cv_w_out, cv_ffn_norm_g, cv_w_gate, cv_w_up, cv_w_down, sw_attn_norm_g, sw_w_in, sw_q_norm_g, sw_k_norm_g, sw_sinks, sw_memq_norm_g, sw_w_out, sw_ffn_norm_g, sw_router, sw_we_gate, sw_we_up, sw_we_down):
    b, s, d = x.shape
    scale = HEAD_DIM ** -0.5
    lane_head = jnp.arange(MEM_WIDTH) // HEAD_DIM
    head_ones = (lane_head[:, None] == lane_head[None, :]).astype(BF16)
    tile4 = lambda gvec: jnp.tile(gvec, MEM_HEADS).reshape(1, MEM_WIDTH)

    kexp, vexp = _mem_kv(mem, mem_norm_g, w_mem_kv.astype(BF16), tile4(mem_k_norm_g), head_ones)

    h = x
    depth = cv_w_in.shape[0] + sw_w_in.shape[0]
    for i in range(depth):
        j = i // 2
        if i % 2 == 0:
            h = _conv_mixer(h, cv_attn_norm_g[j], cv_w_in[j].astype(BF16), cv_b_glu[j], cv_dw_w[j], cv_dw_b[j],
                            cv_ln_g[j], cv_ln_b[j], tile4(cv_memq_norm_g[j]) * scale, head_ones, kexp, vexp,
                            cv_w_out[j].astype(BF16))
            h = _dense_ffn(h.reshape(b * s, d), cv_ffn_norm_g[j], cv_w_gate[j].astype(BF16),
                           cv_w_up[j].astype(BF16), cv_w_down[j].astype(BF16)).reshape(b, s, d)
        else:
            qw = SWA_KV_HEADS * SWA_GROUP * HEAD_DIM
            w_in = jnp.concatenate([_regroup_heads(sw_w_in[j][:, :qw], 1), sw_w_in[j][:, qw:]], axis=1)
            w_out = jnp.concatenate([_regroup_heads(sw_w_out[j][:qw], 0), sw_w_out[j][qw:]], axis=0)
            h = _swa_mixer(h, sw_attn_norm_g[j], w_in.astype(BF16), tile4(sw_q_norm_g[j]) * scale,
                           tile4(sw_k_norm_g[j]), sw_sinks[j], tile4(sw_memq_norm_g[j]) * scale, head_ones,
                           kexp, vexp, w_out.astype(BF16))
            h = _moe(h.reshape(b * s, d), sw_ffn_norm_g[j], sw_router[j], sw_we_gate[j], sw_we_up[j],
                     sw_we_down[j]).reshape(b, s, d)
    return h
```

```python
import jax
import jax.numpy as jnp
from jax import lax
from jax.experimental import pallas as pl
from jax.experimental.pallas import tpu as pltpu

F32 = jnp.float32
BF16 = jnp.bfloat16

HEAD_DIM = 64
MEM_HEADS = 4
MEM_WIDTH = MEM_HEADS * HEAD_DIM
CONV_WIDTH = 31
CONV_HALO = 32
SWA_KV_HEADS = 4
SWA_GROUP = 3
SWA_BLOCK = 128
N_EXPERTS = 8
RMS_EPS = 1e-6
LN_EPS = 1e-5
ALIBI_MAX_BIAS = 8.0
NEG_INF = -1e30
LANES = 128
SUBLANES = 8
VMEM_LIMIT = 56 * 1024 * 1024

TM_MIX = 512
TM_SWA = 1024
TM_FFN = 512
FFN_SPLIT = 1536
TM_ROUTE = 1024
TS_MOE = 512
TF_MOE = 1792
W_STAGE_ROWS = 256
W_STAGE_COLS = 1024
W_STAGE_SLOTS = 6


def _dot(a, b):
    return jnp.dot(a, b, preferred_element_type=F32)


def _dot_nt(a, b):
    return lax.dot_general(a, b, (((1,), (1,)), ((), ())), preferred_element_type=F32)


def _rms_rows(x, g):
    ms = jnp.mean(x * x, axis=-1, keepdims=True)
    return x * lax.rsqrt(ms + RMS_EPS) * g


def _split_bf16(x):
    hi = x.astype(BF16)
    return hi, (x - hi.astype(F32)).astype(BF16)


def _head_mean_sq(x, head_ones):
    hi, lo = _split_bf16(x * x)
    return (_dot(hi, head_ones) + _dot(lo, head_ones)) * (1.0 / HEAD_DIM)


def _head_rms(x, head_ones, g):
    return x * lax.rsqrt(_head_mean_sq(x, head_ones) + RMS_EPS) * g


def _head_lane_mask(h):
    lane = lax.broadcasted_iota(jnp.int32, (1, MEM_WIDTH), 1)
    return ((lane >= h * HEAD_DIM) & (lane < (h + 1) * HEAD_DIM)).astype(F32)


def _mem_attention(q_mem, head_ones, qg_scaled, kexp, vexp):
    qn = _head_rms(q_mem, head_ones, qg_scaled).astype(BF16)
    s = _dot_nt(qn, kexp)
    m_len = kexp.shape[0] // MEM_HEADS
    ps = []
    for h in range(MEM_HEADS):
        sh = s[:, h * m_len:(h + 1) * m_len]
        e = jnp.exp(sh - jnp.max(sh, axis=-1, keepdims=True))
        ps.append((e * (1.0 / jnp.sum(e, axis=-1, keepdims=True))).astype(BF16))
    return _dot(jnp.concatenate(ps, axis=-1), vexp)


def _mem_kv_kernel(mem_ref, g_ref, w_ref, kg_ref, ones_ref, kexp_ref, vexp_ref):
    x = mem_ref[0]
    hn = _rms_rows(x, g_ref[...]).astype(BF16)
    kv = _dot(hn, w_ref[...])
    k = _head_rms(kv[:, :MEM_WIDTH], ones_ref[...], kg_ref[...])
    v = kv[:, MEM_WIDTH:]
    m_len = x.shape[0]
    for h in range(MEM_HEADS):
        mask = _head_lane_mask(h)
        kexp_ref[0, h * m_len:(h + 1) * m_len, :] = (k * mask).astype(BF16)
        vexp_ref[0, h * m_len:(h + 1) * m_len, :] = (v * mask).astype(BF16)


def _mem_kv(mem, g, w_bf, kg_tiled, head_ones):
    b, m_len, d = mem.shape
    out = jax.ShapeDtypeStruct((b, MEM_HEADS * m_len, MEM_WIDTH), BF16)
    const = lambda shape: pl.BlockSpec(shape, lambda i: (0,) * len(shape))
    return pl.pallas_call(
        _mem_kv_kernel,
        out_shape=(out, out),
        grid=(b,),
        in_specs=[pl.BlockSpec((1, m_len, d), lambda i: (i, 0, 0)),
                  const((1, d)), const(w_bf.shape), const((1, MEM_WIDTH)), const(head_ones.shape)],
        out_specs=(pl.BlockSpec((1, MEM_HEADS * m_len, MEM_WIDTH), lambda i: (i, 0, 0)),) * 2,
        name="mem_kv",
    )(mem, g.reshape(1, d), w_bf, kg_tiled, head_ones)


def _conv_mixer_kernel(x_ref, g_ref, win_ref, bglu_ref, dww_ref, dwb_ref, lng_ref, lnb_ref,
                       mqg_ref, ones_ref, kexp_ref, vexp_ref, wout_ref, o_ref, ubuf, cbuf):
    tm = x_ref.shape[1]
    cc = dww_ref.shape[2]

    @pl.when(pl.program_id(1) == 0)
    def _():
        ubuf[...] = jnp.zeros_like(ubuf)

    @pl.when(pl.program_id(1) > 0)
    def _():
        ubuf[:, 0:CONV_HALO, :] = ubuf[:, tm:tm + CONV_HALO, :]

    x = x_ref[0]
    hn = _rms_rows(x, g_ref[...]).astype(BF16)
    proj = _dot(hn, win_ref[...])
    a = proj[:, :cc] + bglu_ref[:, :cc]
    gate = proj[:, cc:2 * cc] + bglu_ref[:, cc:]
    u = a * jax.nn.sigmoid(gate)
    for r in range(SUBLANES):
        ubuf[r, CONV_HALO - r:CONV_HALO - r + tm, :] = u

    rows = 32
    groups = rows // SUBLANES
    base = CONV_HALO - (CONV_WIDTH - 1)
    for r0 in range(0, tm, rows):
        acc = jnp.broadcast_to(dwb_ref[...][None], (groups, SUBLANES, cc))
        for k in range(CONV_WIDTH):
            r = (base + k) % SUBLANES
            lo = r0 + base + k - r
            acc = acc + ubuf[r, lo:lo + rows, :].reshape(groups, SUBLANES, cc) * dww_ref[k][None]
        mu = jnp.mean(acc, axis=-1, keepdims=True)
        dlt = acc - mu
        var = jnp.mean(dlt * dlt, axis=-1, keepdims=True)
        y = dlt * lax.rsqrt(var + LN_EPS) * lng_ref[...][None] + lnb_ref[...][None]
        cbuf[r0:r0 + rows, :] = (y * jax.nn.sigmoid(y)).reshape(rows, cc).astype(BF16)

    m = _mem_attention(proj[:, 2 * cc:], ones_ref[...], mqg_ref[...], kexp_ref[0], vexp_ref[0])
    out = _dot(cbuf[...], wout_ref[0:cc, :]) + _dot(m.astype(BF16), wout_ref[cc:, :])
    o_ref[0] = x + out


def _conv_mixer(h, g, win_bf, bglu, dww, dwb, lng, lnb, mqg_scaled, head_ones, kexp, vexp, wout_bf):
    b, s, d = h.shape
    cc = dww.shape[1]
    tm = TM_MIX
    const = lambda shape: pl.BlockSpec(shape, lambda i, j: (0,) * len(shape))
    per_batch = lambda arr: pl.BlockSpec((1,) + arr.shape[1:], lambda i, j: (i, 0, 0))
    rep = lambda v: jnp.broadcast_to(v[..., None, :], v.shape[:-1] + (SUBLANES, cc))
    return pl.pallas_call(
        _conv_mixer_kernel,
        out_shape=jax.ShapeDtypeStruct(h.shape, F32),
        grid=(b, s // tm),
        in_specs=[pl.BlockSpec((1, tm, d), lambda i, j: (i, j, 0)),
                  const((1, d)), const(win_bf.shape), const((1, 2 * cc)), const((dww.shape[0], SUBLANES, cc)),
                  const((SUBLANES, cc)), const((SUBLANES, cc)), const((SUBLANES, cc)), const((1, MEM_WIDTH)),
                  const(head_ones.shape), per_batch(kexp), per_batch(vexp), const(wout_bf.shape)],
        out_specs=pl.BlockSpec((1, tm, d), lambda i, j: (i, j, 0)),
        scratch_shapes=[pltpu.VMEM((SUBLANES, CONV_HALO + tm, cc), F32), pltpu.VMEM((tm, cc), BF16)],
        compiler_params=pltpu.CompilerParams(dimension_semantics=("arbitrary", "arbitrary"),
                                             vmem_limit_bytes=VMEM_LIMIT),
        name="conv_mixer",
    )(h, g.reshape(1, d), win_bf, bglu.reshape(1, -1), rep(dww), rep(dwb), rep(lng), rep(lnb), mqg_scaled,
      head_ones, kexp, vexp, wout_bf)


def _dense_ffn_kernel(x_ref, g_ref, wg_ref, wu_ref, wd_ref, o_ref):
    x = x_ref[...]
    hn = _rms_rows(x, g_ref[...]).astype(BF16)
    f = wg_ref.shape[1]
    out = x
    for lo, hi in ((0, FFN_SPLIT), (FFN_SPLIT, f)):
        gte = _dot(hn, wg_ref[:, lo:hi])
        up = _dot(hn, wu_ref[:, lo:hi])
        act = (gte * jax.nn.sigmoid(gte) * up).astype(BF16)
        out = out + _dot(act, wd_ref[lo:hi, :])
    o_ref[...] = out


def _dense_ffn(h2d, g, wg_bf, wu_bf, wd_bf):
    t, d = h2d.shape
    tm = TM_FFN
    resident = lambda shape: pl.BlockSpec(shape, lambda i: (0, 0), pipeline_mode=pl.Buffered(1))
    return pl.pallas_call(
        _dense_ffn_kernel,
        out_shape=jax.ShapeDtypeStruct((t, d), F32),
        grid=(t // tm,),
        in_specs=[pl.BlockSpec((tm, d), lambda i: (i, 0)), resident((1, d)),
                  resident(wg_bf.shape), resident(wu_bf.shape), resident(wd_bf.shape)],
        out_specs=pl.BlockSpec((tm, d), lambda i: (i, 0)),
        compiler_params=pltpu.CompilerParams(dimension_semantics=("arbitrary",),
                                             vmem_limit_bytes=VMEM_LIMIT),
        name="dense_ffn",
    )(h2d, g.reshape(1, d), wg_bf, wu_bf, wd_bf)


def _swa_bias_tables():
    blk = SWA_BLOCK
    n_heads = SWA_KV_HEADS * SWA_GROUP
    slopes = jnp.exp2(-ALIBI_MAX_BIAS * (jnp.arange(n_heads, dtype=F32) + 1.0) / n_heads)
    dist = (jnp.arange(blk)[:, None] + blk - jnp.arange(2 * blk)[None, :]).astype(F32)
    window = (dist >= 0) & (dist < blk)
    bias = -slopes[:, None, None] * dist[None]
    current = jnp.arange(2 * blk)[None, :] >= blk
    return jnp.stack([jnp.where(window[None], bias, NEG_INF),
                      jnp.where((window & current)[None], bias, NEG_INF)])


def _swa_mixer_kernel(sinks_ref, x_ref, g_ref, win_ref, qg_ref, kg_ref, mqg_ref, ones_ref, bias_ref,
                      kexp_ref, vexp_ref, wout_ref, o_ref, kbuf, vbuf, obuf):
    tm = x_ref.shape[1]
    blk = SWA_BLOCK
    kvw = SWA_KV_HEADS * HEAD_DIM
    qw = SWA_GROUP * kvw
    first_tile = pl.program_id(1) == 0

    @pl.when(first_tile)
    def _():
        kbuf[:, 0:blk, :] = jnp.zeros((SWA_KV_HEADS, blk, kvw), BF16)
        vbuf[:, 0:blk, :] = jnp.zeros((SWA_KV_HEADS, blk, kvw), BF16)

    x = x_ref[0]
    hn = _rms_rows(x, g_ref[...]).astype(BF16)
    proj = _dot(hn, win_ref[...])
    ones = ones_ref[...]
    kn = _head_rms(proj[:, qw:qw + kvw], ones, kg_ref[...])
    v = proj[:, qw + kvw:qw + 2 * kvw]
    for h in range(SWA_KV_HEADS):
        mask = _head_lane_mask(h)
        kbuf[h, blk:blk + tm, :] = (kn * mask).astype(BF16)
        vbuf[h, blk:blk + tm, :] = (v * mask).astype(BF16)

    qn = [_head_rms(proj[:, g * kvw:(g + 1) * kvw], ones, qg_ref[...]).astype(BF16) for g in range(SWA_GROUP)]
    first_block = jnp.where(first_tile, 1, 0)
    for n in range(tm // blk):
        rows = slice(n * blk, (n + 1) * blk)
        keys = slice(n * blk, (n + 2) * blk)
        table = first_block if n == 0 else 0
        qs = jnp.concatenate([q[rows, :] for q in qn], axis=0)
        og = None
        for h in range(SWA_KV_HEADS):
            s = _dot_nt(qs, kbuf[h, keys, :])
            ps = []
            for g in range(SWA_GROUP):
                head = h * SWA_GROUP + g
                sg = s[g * blk:(g + 1) * blk, :] + bias_ref[table, head]
                sink = sinks_ref[head]
                mx = jnp.maximum(jnp.max(sg, axis=-1, keepdims=True), sink)
                e = jnp.exp(sg - mx)
                den = jnp.sum(e, axis=-1, keepdims=True) + jnp.exp(sink - mx)
                ps.append((e * (1.0 / den)).astype(BF16))
            part = _dot(jnp.concatenate(ps, axis=0), vbuf[h, keys, :])
            og = part if og is None else og + part
        for g in range(SWA_GROUP):
            obuf[rows, g * kvw:(g + 1) * kvw] = og[g * blk:(g + 1) * blk, :].astype(BF16)

    kbuf[:, 0:blk, :] = kbuf[:, tm:tm + blk, :]
    vbuf[:, 0:blk, :] = vbuf[:, tm:tm + blk, :]

    m = _mem_attention(proj[:, qw + 2 * kvw:], ones, mqg_ref[...], kexp_ref[0], vexp_ref[0])
    out = _dot(obuf[...], wout_ref[0:qw, :]) + _dot(m.astype(BF16), wout_ref[qw:, :])
    o_ref[0] = x + out


def _swa_mixer(h, g, win_bf, qg_scaled, kg_tiled, sinks, mqg_scaled, head_ones, kexp, vexp, wout_bf):
    b, s, d = h.shape
    tm = TM_SWA
    kvw = SWA_KV_HEADS * HEAD_DIM
    bias = _swa_bias_tables()
    const = lambda shape: pl.BlockSpec(shape, lambda i, j, sk: (0,) * len(shape))
    per_batch = lambda arr: pl.BlockSpec((1,) + arr.shape[1:], lambda i, j, sk: (i, 0, 0))
    grid_spec = pltpu.PrefetchScalarGridSpec(
        num_scalar_prefetch=1,
        grid=(b, s // tm),
        in_specs=[pl.BlockSpec((1, tm, d), lambda i, j, sk: (i, j, 0)),
                  const((1, d)), const(win_bf.shape), const((1, kvw)), const((1, kvw)),
                  const((1, MEM_WIDTH)), const(head_ones.shape), const(bias.shape), per_batch(kexp),
                  per_batch(vexp), const(wout_bf.shape)],
        out_specs=pl.BlockSpec((1, tm, d), lambda i, j, sk: (i, j, 0)),
        scratch_shapes=[pltpu.VMEM((SWA_KV_HEADS, SWA_BLOCK + tm, kvw), BF16),
                        pltpu.VMEM((SWA_KV_HEADS, SWA_BLOCK + tm, kvw), BF16),
                        pltpu.VMEM((tm, SWA_GROUP * kvw), BF16)])
    return pl.pallas_call(
        _swa_mixer_kernel,
        out_shape=jax.ShapeDtypeStruct(h.shape, F32),
        grid_spec=grid_spec,
        compiler_params=pltpu.CompilerParams(dimension_semantics=("arbitrary", "arbitrary"),
                                             vmem_limit_bytes=VMEM_LIMIT),
        name="swa_mixer",
    )(sinks, h, g.reshape(1, d), win_bf, qg_scaled, kg_tiled, mqg_scaled, head_ones, bias, kexp, vexp, wout_bf)


def _router_kernel(x_ref, g_ref, r_ref, info_ref, cnt_ref, run):
    tm = x_ref.shape[0]

    @pl.when(pl.program_id(0) == 0)
    def _():
        run[...] = jnp.zeros_like(run)

    hn_hi, hn_lo = _split_bf16(_rms_rows(x_ref[...], g_ref[...]))
    r_hi, r_lo = _split_bf16(r_ref[...])
    logits = _dot(hn_hi, r_hi) + (_dot(hn_hi, r_lo) + _dot(hn_lo, r_hi))
    lane = lax.broadcasted_iota(jnp.int32, (tm, LANES), 1).astype(F32)
    logits = jnp.where(lane < N_EXPERTS, logits, -jnp.inf)
    m1 = jnp.max(logits, axis=-1, keepdims=True)
    e1 = jnp.min(jnp.where(logits == m1, lane, float(LANES)), axis=-1, keepdims=True)
    oh1 = lane == e1
    rest = jnp.where(oh1, -jnp.inf, logits)
    m2 = jnp.max(rest, axis=-1, keepdims=True)
    e2 = jnp.min(jnp.where(rest == m2, lane, float(LANES)), axis=-1, keepdims=True)
    oh2 = lane == e2
    z = jnp.exp(m2 - m1)
    w1 = 1.0 / (1.0 + z)
    w2 = z / (1.0 + z)

    both = jnp.where(oh1 | oh2, 1.0, 0.0)
    ri = lax.broadcasted_iota(jnp.int32, (tm, tm), 0)
    ci = lax.broadcasted_iota(jnp.int32, (tm, tm), 1)
    before = jnp.where(ci < ri, 1.0, 0.0).astype(BF16)
    cnt = _dot(before, both.astype(BF16)) + run[...]
    r1 = jnp.sum(jnp.where(oh1, cnt, 0.0), axis=-1, keepdims=True)
    r2 = jnp.sum(jnp.where(oh2, cnt, 0.0), axis=-1, keepdims=True)
    run[...] = run[...] + jnp.sum(both, axis=0, keepdims=True)
    cnt_ref[...] = run[...]

    info = jnp.zeros((tm, LANES), F32)
    for idx, val in enumerate((e1, e2, r1, r2, w1, w2)):
        info = jnp.where(lane == idx, val, info)
    info_ref[...] = info


def _router(h2d, g, router_pad):
    t, d = h2d.shape
    tm = TM_ROUTE
    const = lambda shape: pl.BlockSpec(shape, lambda i: (0, 0))
    return pl.pallas_call(
        _router_kernel,
        out_shape=(jax.ShapeDtypeStruct((t, LANES), F32), jax.ShapeDtypeStruct((1, LANES), F32)),
        grid=(t // tm,),
        in_specs=[pl.BlockSpec((tm, d), lambda i: (i, 0)), const((1, d)), const(router_pad.shape)],
        out_specs=(pl.BlockSpec((tm, LANES), lambda i: (i, 0)), const((1, LANES))),
        scratch_shapes=[pltpu.VMEM((1, LANES), F32)],
        compiler_params=pltpu.CompilerParams(dimension_semantics=("arbitrary",)),
        name="router",
    )(h2d, g.reshape(1, d), router_pad)


def _row_copy(src, src_row, dst, dst_row, sem):
    s = pl.ds(pl.multiple_of(src_row * SUBLANES, SUBLANES), SUBLANES)
    d = pl.ds(pl.multiple_of(dst_row * SUBLANES, SUBLANES), SUBLANES)
    return pltpu.make_async_copy(src.at[s], dst.at[d], sem)


def _store_tile_rows(ref, x):
    n = x.shape[0]
    for c in range(SUBLANES):
        ref[pl.ds(c, n, stride=SUBLANES), :] = x[:, c * LANES:(c + 1) * LANES]


def _load_tile_rows(ref):
    n = ref.shape[0] // SUBLANES
    return jnp.concatenate([ref[pl.ds(c, n, stride=SUBLANES), :] for c in range(SUBLANES)], axis=-1)


def _dispatch_kernel(slot_ref, end_ref, x_ref, g_ref, xs_hbm, hbuf, zbuf, sem, zsem):
    tm = x_ref.shape[0]
    ts = zbuf.shape[0] // SUBLANES
    i = pl.program_id(0)

    @pl.when(i == 0)
    def _():
        zbuf[...] = jnp.zeros_like(zbuf)

        def zero_tile(first_slot):
            rows = pl.ds(pl.multiple_of(first_slot * SUBLANES, SUBLANES), ts * SUBLANES)
            block = pltpu.make_async_copy(zbuf, xs_hbm.at[rows], zsem)
            block.start()
            block.wait()

        for e in range(N_EXPERTS):
            zero_tile(end_ref[e])
        n_tiles = xs_hbm.shape[0] // (ts * SUBLANES)
        for q in range(n_tiles - slot_ref.shape[0] // ts):
            tile = end_ref[N_EXPERTS] + q

            @pl.when(tile < n_tiles)
            def _():
                zero_tile(tile * ts)

    def drain(buf):
        def wait(r, c):
            for k in range(2):
                _row_copy(hbuf.at[buf], r, xs_hbm, 0, sem.at[buf]).wait()
            return c
        lax.fori_loop(0, tm, wait, 0, unroll=8)

    buf = i % 2

    @pl.when(i >= 2)
    def _():
        drain(buf)

    _store_tile_rows(hbuf.at[buf], _rms_rows(x_ref[...], g_ref[...]))

    def issue(r, c):
        for k in range(2):
            _row_copy(hbuf.at[buf], r, xs_hbm, slot_ref[2 * (i * tm + r) + k], sem.at[buf]).start(priority=k)
        return c
    lax.fori_loop(0, tm, issue, 0, unroll=8)

    n_steps = slot_ref.shape[0] // (2 * tm)

    @pl.when(i == n_steps - 1)
    def _():
        if n_steps > 1:
            drain(1 - buf)
        drain(buf)


def _dispatch(slots, ends, h2d, g, n_slots):
    t, d = h2d.shape
    assert d == SUBLANES * LANES, "a slot row must fill exactly one (8, 128) tile"
    tm = TM_ROUTE
    grid_spec = pltpu.PrefetchScalarGridSpec(
        num_scalar_prefetch=2,
        grid=(t // tm,),
        in_specs=[pl.BlockSpec((tm, d), lambda i, sl, cn: (i, 0)), pl.BlockSpec((1, d), lambda i, sl, cn: (0, 0))],
        out_specs=pl.BlockSpec(memory_space=pl.ANY),
        scratch_shapes=[pltpu.VMEM((2, tm * SUBLANES, LANES), F32), pltpu.VMEM((TS_MOE * SUBLANES, LANES), F32),
                        pltpu.SemaphoreType.DMA((2,)), pltpu.SemaphoreType.DMA])
    return pl.pallas_call(
        _dispatch_kernel,
        out_shape=jax.ShapeDtypeStruct((n_slots * SUBLANES, LANES), F32),
        grid_spec=grid_spec,
        compiler_params=pltpu.CompilerParams(dimension_semantics=("arbitrary",)),
        name="moe_dispatch",
    )(slots, ends, h2d, g.reshape(1, d))


def _moe_ffn_kernel(te_ref, nvt_ref, xs_ref, wg_hbm, wu_hbm, wd_hbm, y_ref, wg, wu, wd, stage, sem):
    i = pl.program_id(0)
    valid = i < nvt_ref[0]
    expert = te_ref[i]
    f = wg.shape[1]
    n_slot, rows, cols = stage.shape

    @pl.when(jnp.logical_not(valid))
    def _():
        y_ref[...] = jnp.zeros_like(y_ref)

    pieces = [(src, dst, r0, c0, min(cols, dst.shape[1] - c0))
              for src, dst in ((wg_hbm, wg), (wu_hbm, wu), (wd_hbm, wd))
              for r0 in range(0, dst.shape[0], rows) for c0 in range(0, dst.shape[1], cols)]

    def copy(p):
        src, _, r0, c0, width = pieces[p]
        slot = p % n_slot
        return pltpu.make_async_copy(src.at[expert, pl.ds(r0, rows), pl.ds(c0, width)],
                                     stage.at[slot, :, pl.ds(0, width)], sem.at[slot])

    @pl.when(valid & ((i == 0) | (expert != te_ref[jnp.maximum(i - 1, 0)])))
    def _():
        for p in range(min(n_slot - 1, len(pieces))):
            copy(p).start()
        for p in range(len(pieces)):
            if p + n_slot - 1 < len(pieces):
                copy(p + n_slot - 1).start()
            copy(p).wait()
            _, dst, r0, c0, width = pieces[p]
            dst[r0:r0 + rows, c0:c0 + width] = stage[p % n_slot, :, 0:width].astype(BF16)

    @pl.when(valid)
    def _():
        x = _load_tile_rows(xs_ref).astype(BF16)
        out = None
        for lo in range(0, f, TF_MOE):
            gte = _dot(x, wg[:, lo:lo + TF_MOE])
            up = _dot(x, wu[:, lo:lo + TF_MOE])
            act = (gte * jax.nn.sigmoid(gte) * up).astype(BF16)
            part = _dot(act, wd[lo:lo + TF_MOE, :])
            out = part if out is None else out + part
        _store_tile_rows(y_ref, out)


def _moe_ffn(tile_expert, n_valid_tiles, xs, wg, wu, wd):
    n_exp, d, f = wg.shape
    ts = TS_MOE
    n_tiles = tile_expert.shape[0]
    hbm = pl.BlockSpec(memory_space=pl.ANY)
    tile_rows = pl.BlockSpec((ts * SUBLANES, LANES), lambda i, te, nvt: (i, 0))
    grid_spec = pltpu.PrefetchScalarGridSpec(
        num_scalar_prefetch=2,
        grid=(n_tiles,),
        in_specs=[tile_rows, hbm, hbm, hbm],
        out_specs=tile_rows,
        scratch_shapes=[pltpu.VMEM((d, f), BF16), pltpu.VMEM((d, f), BF16), pltpu.VMEM((f, d), BF16),
                        pltpu.VMEM((W_STAGE_SLOTS, W_STAGE_ROWS, W_STAGE_COLS), F32),
                        pltpu.SemaphoreType.DMA((W_STAGE_SLOTS,))])
    return pl.pallas_call(
        _moe_ffn_kernel,
        out_shape=jax.ShapeDtypeStruct((n_tiles * ts * SUBLANES, LANES), F32),
        grid_spec=grid_spec,
        compiler_params=pltpu.CompilerParams(dimension_semantics=("arbitrary",),
                                             vmem_limit_bytes=VMEM_LIMIT),
        name="moe_ffn",
    )(tile_expert, n_valid_tiles, xs, wg, wu, wd)


def _combine_kernel(slot_ref, h_ref, info_ref, y_hbm, o_ref, ybuf, sem):
    tm, d = h_ref.shape
    i = pl.program_id(0)
    n = pl.num_programs(0)

    def fetch(tile, start):
        buf = tile % 2

        def body(r, c):
            for k in range(2):
                cp = _row_copy(y_hbm, slot_ref[2 * (tile * tm + r) + k] if start else 0, ybuf.at[buf, k], r,
                               sem.at[buf])
                cp.start(priority=k) if start else cp.wait()
            return c
        lax.fori_loop(0, tm, body, 0, unroll=8)

    @pl.when(i == 0)
    def _():
        fetch(i, True)

    @pl.when(i + 1 < n)
    def _():
        fetch(i + 1, True)

    fetch(i, False)
    buf = i % 2
    w1 = info_ref[:, 4:5]
    w2 = info_ref[:, 5:6]
    o_ref[...] = h_ref[...] + (w1 * _load_tile_rows(ybuf.at[buf, 0]) + w2 * _load_tile_rows(ybuf.at[buf, 1]))


def _combine(slots, h2d, info, y):
    t, d = h2d.shape
    tm = TM_ROUTE
    grid_spec = pltpu.PrefetchScalarGridSpec(
        num_scalar_prefetch=1,
        grid=(t // tm,),
        in_specs=[pl.BlockSpec((tm, d), lambda i, sl: (i, 0)), pl.BlockSpec((tm, LANES), lambda i, sl: (i, 0)),
                  pl.BlockSpec(memory_space=pl.ANY)],
        out_specs=pl.BlockSpec((tm, d), lambda i, sl: (i, 0)),
        scratch_shapes=[pltpu.VMEM((2, 2, tm * SUBLANES, LANES), F32), pltpu.SemaphoreType.DMA((2,))])
    return pl.pallas_call(
        _combine_kernel,
        out_shape=jax.ShapeDtypeStruct((t, d), F32),
        grid_spec=grid_spec,
        compiler_params=pltpu.CompilerParams(dimension_semantics=("arbitrary",)),
        name="moe_combine",
    )(slots, h2d, info, y)


def _moe(h2d, g, router, wg, wu, wd):
    t, d = h2d.shape
    ts = TS_MOE
    router_pad = jnp.pad(router, ((0, 0), (0, LANES - N_EXPERTS)))
    info, counts = _router(h2d, g, router_pad)

    counts = counts[0, :N_EXPERTS].astype(jnp.int32)
    tiles_per_expert = (counts + ts - 1) // ts
    tile_end = jnp.cumsum(tiles_per_expert)
    starts = (tile_end - tiles_per_expert) * ts
    n_valid = tile_end[-1]
    chosen = info[:, 0:2].astype(jnp.int32)[..., None] == jnp.arange(N_EXPERTS, dtype=jnp.int32)
    slots = (jnp.sum(jnp.where(chosen, starts, 0), axis=-1) + info[:, 2:4].astype(jnp.int32)).reshape(-1)
    n_tiles = 2 * t // ts + N_EXPERTS
    tile = jnp.minimum(jnp.arange(n_tiles, dtype=jnp.int32), n_valid - 1)
    tile_expert = jnp.sum((tile[:, None] >= tile_end[None, :]).astype(jnp.int32), axis=1)
    ends = jnp.concatenate([starts + counts, n_valid.reshape(1)])

    xs = _dispatch(slots, ends, h2d, g, (n_tiles + 1) * ts)
    y = _moe_ffn(tile_expert, n_valid.reshape(1), xs, wg, wu, wd)
    return _combine(slots, h2d, info, y)


def _regroup_heads(w, axis):
    shape = w.shape
    split = shape[:axis] + (SWA_KV_HEADS, SWA_GROUP, HEAD_DIM) + shape[axis + 1:]
    return jnp.swapaxes(w.reshape(split), axis, axis + 1).reshape(shape)


def kernel(x, mem, mem_norm_g, w_mem_kv, mem_k_norm_g, cv_attn_norm_g, cv_w_in, cv_b_glu, cv_dw_w, cv_dw_b, cv_ln_g, cv_ln_b, cv_memq_norm_g, cv_w_out, cv_ffn_norm_g, cv_w_gate, cv_w_up, cv_w_down, sw_attn_norm_g, sw_w_in, sw_q_norm_g, sw_k_norm_g, sw_sinks, sw_memq_norm_g, sw_w_out, sw_ffn_norm_g, sw_router, sw_we_gate, sw_we_up, sw_we_down):
    b, s, d = x.shape
    scale = HEAD_DIM ** -0.5
    lane_head = jnp.arange(MEM_WIDTH) // HEAD_DIM
    head_ones = (lane_head[:, None] == lane_head[None, :]).astype(BF16)
    tile4 = lambda gvec: jnp.tile(gvec, MEM_HEADS).reshape(1, MEM_WIDTH)

    kexp, vexp = _mem_kv(mem, mem_norm_g, w_mem_kv.astype(BF16), tile4(mem_k_norm_g), head_ones)

    h = x
    depth = cv_w_in.shape[0] + sw_w_in.shape[0]
    for i in range(depth):
        j = i // 2
        if i % 2 == 0:
            h = _conv_mixer(h, cv_attn_norm_g[j], cv_w_in[j].astype(BF16), cv_b_glu[j], cv_dw_w[j], cv_dw_b[j],
                            cv_ln_g[j], cv_ln_b[j], tile4(cv_memq_norm_g[j]) * scale, head_ones, kexp, vexp,
                            cv_w_out[j].astype(BF16))
            h = _dense_ffn(h.reshape(b * s, d), cv_ffn_norm_g[j], cv_w_gate[j].astype(BF16),
                           cv_w_up[j].astype(BF16), cv_w_down[j].astype(BF16)).reshape(b, s, d)
        else:
            qw = SWA_KV_HEADS * SWA_GROUP * HEAD_DIM
            w_in = jnp.concatenate([_regroup_heads(sw_w_in[j][:, :qw], 1), sw_w_in[j][:, qw:]], axis=1)
            w_out = jnp.concatenate([_regroup_heads(sw_w_out[j][:qw], 0), sw_w_out[j][qw:]], axis=0)
            h = _swa_mixer(h, sw_attn_norm_g[j], w_in.astype(BF16), tile4(sw_q_norm_g[j]) * scale,
                           tile4(sw_k_norm_g[j]), sw_sinks[j], tile4(sw_memq_norm_g[j]) * scale, head_ones,
                           kexp, vexp, w_out.astype(BF16))
            h = _moe(h.reshape(b * s, d), sw_ffn_norm_g[j], sw_router[j], sw_we_gate[j], sw_we_up[j],
                     sw_we_down[j]).reshape(b, s, d)
    return h
```

```python
import jax
import jax.numpy as jnp
from jax import lax
from jax.experimental import pallas as pl
from jax.experimental.pallas import tpu as pltpu

F32 = jnp.float32
BF16 = jnp.bfloat16

HEAD_DIM = 64
MEM_HEADS = 4
MEM_WIDTH = MEM_HEADS * HEAD_DIM
CONV_WIDTH = 31
CONV_HALO = 32
SWA_KV_HEADS = 4
SWA_GROUP = 3
SWA_BLOCK = 128
N_EXPERTS = 8
RMS_EPS = 1e-6
LN_EPS = 1e-5
ALIBI_MAX_BIAS = 8.0
NEG_INF = -1e30
LANES = 128
SUBLANES = 8
VMEM_LIMIT = 56 * 1024 * 1024

TM_MIX = 512
TM_SWA = 1024
TM_FFN = 512
FFN_SPLIT = 1536
TM_ROUTE = 1024
TM_COMBINE = 512
MOE_ROW_STEP = 128
TS_MOE = 512
TF_MOE = 1792
W_STAGE_ROWS = 256
W_STAGE_COLS = 1024
W_STAGE_SLOTS = 6


def _dot(a, b):
    return jnp.dot(a, b, preferred_element_type=F32)


def _dot_nt(a, b):
    return lax.dot_general(a, b, (((1,), (1,)), ((), ())), preferred_element_type=F32)


def _rms_rows(x, g):
    ms = jnp.mean(x * x, axis=-1, keepdims=True)
    return x * lax.rsqrt(ms + RMS_EPS) * g


def _split_bf16(x):
    hi = x.astype(BF16)
    return hi, (x - hi.astype(F32)).astype(BF16)


def _head_mean_sq(x, head_ones):
    hi, lo = _split_bf16(x * x)
    return (_dot(hi, head_ones) + _dot(lo, head_ones)) * (1.0 / HEAD_DIM)


def _head_rms(x, head_ones, g):
    return x * lax.rsqrt(_head_mean_sq(x, head_ones) + RMS_EPS) * g


def _head_lane_mask(h):
    lane = lax.broadcasted_iota(jnp.int32, (1, MEM_WIDTH), 1)
    return ((lane >= h * HEAD_DIM) & (lane < (h + 1) * HEAD_DIM)).astype(F32)


def _mem_attention(q_mem, head_ones, qg_scaled, kexp, vexp):
    qn = _head_rms(q_mem, head_ones, qg_scaled).astype(BF16)
    s = _dot_nt(qn, kexp)
    m_len = kexp.shape[0] // MEM_HEADS
    ps = []
    for h in range(MEM_HEADS):
        sh = s[:, h * m_len:(h + 1) * m_len]
        e = jnp.exp(sh - jnp.max(sh, axis=-1, keepdims=True))
        ps.append((e * (1.0 / jnp.sum(e, axis=-1, keepdims=True))).astype(BF16))
    return _dot(jnp.concatenate(ps, axis=-1), vexp)


def _mem_kv_kernel(mem_ref, g_ref, w_ref, kg_ref, ones_ref, kexp_ref, vexp_ref):
    x = mem_ref[0]
    hn = _rms_rows(x, g_ref[...]).astype(BF16)
    kv = _dot(hn, w_ref[...])
    k = _head_rms(kv[:, :MEM_WIDTH], ones_ref[...], kg_ref[...])
    v = kv[:, MEM_WIDTH:]
    m_len = x.shape[0]
    for h in range(MEM_HEADS):
        mask = _head_lane_mask(h)
        kexp_ref[0, h * m_len:(h + 1) * m_len, :] = (k * mask).astype(BF16)
        vexp_ref[0, h * m_len:(h + 1) * m_len, :] = (v * mask).astype(BF16)


def _mem_kv(mem, g, w_bf, kg_tiled, head_ones):
    b, m_len, d = mem.shape
    out = jax.ShapeDtypeStruct((b, MEM_HEADS * m_len, MEM_WIDTH), BF16)
    const = lambda shape: pl.BlockSpec(shape, lambda i: (0,) * len(shape))
    return pl.pallas_call(
        _mem_kv_kernel,
        out_shape=(out, out),
        grid=(b,),
        in_specs=[pl.BlockSpec((1, m_len, d), lambda i: (i, 0, 0)),
                  const((1, d)), const(w_bf.shape), const((1, MEM_WIDTH)), const(head_ones.shape)],
        out_specs=(pl.BlockSpec((1, MEM_HEADS * m_len, MEM_WIDTH), lambda i: (i, 0, 0)),) * 2,
        name="mem_kv",
    )(mem, g.reshape(1, d), w_bf, kg_tiled, head_ones)


def _conv_mixer_kernel(x_ref, g_ref, win_ref, bglu_ref, dww_ref, dwb_ref, lng_ref, lnb_ref,
                       mqg_ref, ones_ref, kexp_ref, vexp_ref, wout_ref, o_ref, ubuf, cbuf):
    tm = x_ref.shape[1]
    cc = dww_ref.shape[2]

    @pl.when(pl.program_id(1) == 0)
    def _():
        ubuf[...] = jnp.zeros_like(ubuf)

    @pl.when(pl.program_id(1) > 0)
    def _():
        ubuf[:, 0:CONV_HALO, :] = ubuf[:, tm:tm + CONV_HALO, :]

    x = x_ref[0]
    hn = _rms_rows(x, g_ref[...]).astype(BF16)
    proj = _dot(hn, win_ref[...])
    a = proj[:, :cc] + bglu_ref[:, :cc]
    gate = proj[:, cc:2 * cc] + bglu_ref[:, cc:]
    u = a * jax.nn.sigmoid(gate)
    for r in range(SUBLANES):
        ubuf[r, CONV_HALO - r:CONV_HALO - r + tm, :] = u

    rows = 32
    groups = rows // SUBLANES
    base = CONV_HALO - (CONV_WIDTH - 1)
    for r0 in range(0, tm, rows):
        acc = jnp.broadcast_to(dwb_ref[...][None], (groups, SUBLANES, cc))
        for k in range(CONV_WIDTH):
            r = (base + k) % SUBLANES
            lo = r0 + base + k - r
            acc = acc + ubuf[r, lo:lo + rows, :].reshape(groups, SUBLANES, cc) * dww_ref[k][None]
        mu = jnp.mean(acc, axis=-1, keepdims=True)
        dlt = acc - mu
        var = jnp.mean(dlt * dlt, axis=-1, keepdims=True)
        y = dlt * lax.rsqrt(var + LN_EPS) * lng_ref[...][None] + lnb_ref[...][None]
        cbuf[r0:r0 + rows, :] = (y * jax.nn.sigmoid(y)).reshape(rows, cc).astype(BF16)

    m = _mem_attention(proj[:, 2 * cc:], ones_ref[...], mqg_ref[...], kexp_ref[0], vexp_ref[0])
    out = _dot(cbuf[...], wout_ref[0:cc, :]) + _dot(m.astype(BF16), wout_ref[cc:, :])
    o_ref[0] = x + out


def _conv_mixer(h, g, win_bf, bglu, dww, dwb, lng, lnb, mqg_scaled, head_ones, kexp, vexp, wout_bf):
    b, s, d = h.shape
    cc = dww.shape[1]
    tm = TM_MIX
    const = lambda shape: pl.BlockSpec(shape, lambda i, j: (0,) * len(shape))
    per_batch = lambda arr: pl.BlockSpec((1,) + arr.shape[1:], lambda i, j: (i, 0, 0))
    rep = lambda v: jnp.broadcast_to(v[..., None, :], v.shape[:-1] + (SUBLANES, cc))
    return pl.pallas_call(
        _conv_mixer_kernel,
        out_shape=jax.ShapeDtypeStruct(h.shape, F32),
        grid=(b, s // tm),
        in_specs=[pl.BlockSpec((1, tm, d), lambda i, j: (i, j, 0)),
                  const((1, d)), const(win_bf.shape), const((1, 2 * cc)), const((dww.shape[0], SUBLANES, cc)),
                  const((SUBLANES, cc)), const((SUBLANES, cc)), const((SUBLANES, cc)), const((1, MEM_WIDTH)),
                  const(head_ones.shape), per_batch(kexp), per_batch(vexp), const(wout_bf.shape)],
        out_specs=pl.BlockSpec((1, tm, d), lambda i, j: (i, j, 0)),
        scratch_shapes=[pltpu.VMEM((SUBLANES, CONV_HALO + tm, cc), F32), pltpu.VMEM((tm, cc), BF16)],
        compiler_params=pltpu.CompilerParams(dimension_semantics=("arbitrary", "arbitrary"),
                                             vmem_limit_bytes=VMEM_LIMIT),
        name="conv_mixer",
    )(h, g.reshape(1, d), win_bf, bglu.reshape(1, -1), rep(dww), rep(dwb), rep(lng), rep(lnb), mqg_scaled,
      head_ones, kexp, vexp, wout_bf)


def _dense_ffn_kernel(x_ref, g_ref, wg_ref, wu_ref, wd_ref, o_ref):
    x = x_ref[...]
    hn = _rms_rows(x, g_ref[...]).astype(BF16)
    f = wg_ref.shape[1]
    out = x
    for lo, hi in ((0, FFN_SPLIT), (FFN_SPLIT, f)):
        gte = _dot(hn, wg_ref[:, lo:hi])
        up = _dot(hn, wu_ref[:, lo:hi])
        act = (gte * jax.nn.sigmoid(gte) * up).astype(BF16)
        out = out + _dot(act, wd_ref[lo:hi, :])
    o_ref[...] = out


def _dense_ffn(h2d, g, wg_bf, wu_bf, wd_bf):
    t, d = h2d.shape
    tm = TM_FFN
    resident = lambda shape: pl.BlockSpec(shape, lambda i: (0, 0), pipeline_mode=pl.Buffered(1))
    return pl.pallas_call(
        _dense_ffn_kernel,
        out_shape=jax.ShapeDtypeStruct((t, d), F32),
        grid=(t // tm,),
        in_specs=[pl.BlockSpec((tm, d), lambda i: (i, 0)), resident((1, d)),
                  resident(wg_bf.shape), resident(wu_bf.shape), resident(wd_bf.shape)],
        out_specs=pl.BlockSpec((tm, d), lambda i: (i, 0)),
        compiler_params=pltpu.CompilerParams(dimension_semantics=("arbitrary",),
                                             vmem_limit_bytes=VMEM_LIMIT),
        name="dense_ffn",
    )(h2d, g.reshape(1, d), wg_bf, wu_bf, wd_bf)


def _swa_bias_tables():
    blk = SWA_BLOCK
    n_heads = SWA_KV_HEADS * SWA_GROUP
    slopes = jnp.exp2(-ALIBI_MAX_BIAS * (jnp.arange(n_heads, dtype=F32) + 1.0) / n_heads)
    dist = (jnp.arange(blk)[:, None] + blk - jnp.arange(2 * blk)[None, :]).astype(F32)
    window = (dist >= 0) & (dist < blk)
    bias = -slopes[:, None, None] * dist[None]
    current = jnp.arange(2 * blk)[None, :] >= blk
    return jnp.stack([jnp.where(window[None], bias, NEG_INF),
                      jnp.where((window & current)[None], bias, NEG_INF)])


def _swa_mixer_kernel(sinks_ref, x_ref, g_ref, win_ref, qg_ref, kg_ref, mqg_ref, ones_ref, bias_ref,
                      kexp_ref, vexp_ref, wout_ref, o_ref, kbuf, vbuf, obuf):
    tm = x_ref.shape[1]
    blk = SWA_BLOCK
    kvw = SWA_KV_HEADS * HEAD_DIM
    qw = SWA_GROUP * kvw
    first_tile = pl.program_id(1) == 0

    @pl.when(first_tile)
    def _():
        kbuf[:, 0:blk, :] = jnp.zeros((SWA_KV_HEADS, blk, kvw), BF16)
        vbuf[:, 0:blk, :] = jnp.zeros((SWA_KV_HEADS, blk, kvw), BF16)

    x = x_ref[0]
    hn = _rms_rows(x, g_ref[...]).astype(BF16)
    proj = _dot(hn, win_ref[...])
    ones = ones_ref[...]
    kn = _head_rms(proj[:, qw:qw + kvw], ones, kg_ref[...])
    v = proj[:, qw + kvw:qw + 2 * kvw]
    for h in range(SWA_KV_HEADS):
        mask = _head_lane_mask(h)
        kbuf[h, blk:blk + tm, :] = (kn * mask).astype(BF16)
        vbuf[h, blk:blk + tm, :] = (v * mask).astype(BF16)

    qn = [_head_rms(proj[:, g * kvw:(g + 1) * kvw], ones, qg_ref[...]).astype(BF16) for g in range(SWA_GROUP)]
    first_block = jnp.where(first_tile, 1, 0)
    for n in range(tm // blk):
        rows = slice(n * blk, (n + 1) * blk)
        keys = slice(n * blk, (n + 2) * blk)
        table = first_block if n == 0 else 0
        qs = jnp.concatenate([q[rows, :] for q in qn], axis=0)
        og = None
        for h in range(SWA_KV_HEADS):
            s = _dot_nt(qs, kbuf[h, keys, :])
            ps = []
            for g in range(SWA_GROUP):
                head = h * SWA_GROUP + g
                sg = s[g * blk:(g + 1) * blk, :] + bias_ref[table, head]
                sink = sinks_ref[head]
                mx = jnp.maximum(jnp.max(sg, axis=-1, keepdims=True), sink)
                e = jnp.exp(sg - mx)
                den = jnp.sum(e, axis=-1, keepdims=True) + jnp.exp(sink - mx)
                ps.append((e * (1.0 / den)).astype(BF16))
            part = _dot(jnp.concatenate(ps, axis=0), vbuf[h, keys, :])
            og = part if og is None else og + part
        for g in range(SWA_GROUP):
            obuf[rows, g * kvw:(g + 1) * kvw] = og[g * blk:(g + 1) * blk, :].astype(BF16)

    kbuf[:, 0:blk, :] = kbuf[:, tm:tm + blk, :]
    vbuf[:, 0:blk, :] = vbuf[:, tm:tm + blk, :]

    m = _mem_attention(proj[:, qw + 2 * kvw:], ones, mqg_ref[...], kexp_ref[0], vexp_ref[0])
    out = _dot(obuf[...], wout_ref[0:qw, :]) + _dot(m.astype(BF16), wout_ref[qw:, :])
    o_ref[0] = x + out


def _swa_mixer(h, g, win_bf, qg_scaled, kg_tiled, sinks, mqg_scaled, head_ones, kexp, vexp, wout_bf):
    b, s, d = h.shape
    tm = TM_SWA
    kvw = SWA_KV_HEADS * HEAD_DIM
    bias = _swa_bias_tables()
    const = lambda shape: pl.BlockSpec(shape, lambda i, j, sk: (0,) * len(shape))
    per_batch = lambda arr: pl.BlockSpec((1,) + arr.shape[1:], lambda i, j, sk: (i, 0, 0))
    grid_spec = pltpu.PrefetchScalarGridSpec(
        num_scalar_prefetch=1,
        grid=(b, s // tm),
        in_specs=[pl.BlockSpec((1, tm, d), lambda i, j, sk: (i, j, 0)),
                  const((1, d)), const(win_bf.shape), const((1, kvw)), const((1, kvw)),
                  const((1, MEM_WIDTH)), const(head_ones.shape), const(bias.shape), per_batch(kexp),
                  per_batch(vexp), const(wout_bf.shape)],
        out_specs=pl.BlockSpec((1, tm, d), lambda i, j, sk: (i, j, 0)),
        scratch_shapes=[pltpu.VMEM((SWA_KV_HEADS, SWA_BLOCK + tm, kvw), BF16),
                        pltpu.VMEM((SWA_KV_HEADS, SWA_BLOCK + tm, kvw), BF16),
                        pltpu.VMEM((tm, SWA_GROUP * kvw), BF16)])
    return pl.pallas_call(
        _swa_mixer_kernel,
        out_shape=jax.ShapeDtypeStruct(h.shape, F32),
        grid_spec=grid_spec,
        compiler_params=pltpu.CompilerParams(dimension_semantics=("arbitrary", "arbitrary"),
                                             vmem_limit_bytes=VMEM_LIMIT),
        name="swa_mixer",
    )(sinks, h, g.reshape(1, d), win_bf, qg_scaled, kg_tiled, mqg_scaled, head_ones, bias, kexp, vexp, wout_bf)


def _router_kernel(x_ref, g_ref, r_ref, info_ref, cnt_ref, run):
    tm = x_ref.shape[0]

    @pl.when(pl.program_id(0) == 0)
    def _():
        run[...] = jnp.zeros_like(run)

    hn_hi, hn_lo = _split_bf16(_rms_rows(x_ref[...], g_ref[...]))
    r_hi, r_lo = _split_bf16(r_ref[...])
    logits = _dot(hn_hi, r_hi) + (_dot(hn_hi, r_lo) + _dot(hn_lo, r_hi))
    lane = lax.broadcasted_iota(jnp.int32, (tm, LANES), 1).astype(F32)
    logits = jnp.where(lane < N_EXPERTS, logits, -jnp.inf)
    m1 = jnp.max(logits, axis=-1, keepdims=True)
    e1 = jnp.min(jnp.where(logits == m1, lane, float(LANES)), axis=-1, keepdims=True)
    oh1 = lane == e1
    rest = jnp.where(oh1, -jnp.inf, logits)
    m2 = jnp.max(rest, axis=-1, keepdims=True)
    e2 = jnp.min(jnp.where(rest == m2, lane, float(LANES)), axis=-1, keepdims=True)
    oh2 = lane == e2
    z = jnp.exp(m2 - m1)
    w1 = 1.0 / (1.0 + z)
    w2 = z / (1.0 + z)

    both = jnp.where(oh1 | oh2, 1.0, 0.0)
    ri = lax.broadcasted_iota(jnp.int32, (tm, tm), 0)
    ci = lax.broadcasted_iota(jnp.int32, (tm, tm), 1)
    before = jnp.where(ci < ri, 1.0, 0.0).astype(BF16)
    cnt = _dot(before, both.astype(BF16)) + run[...]
    r1 = jnp.sum(jnp.where(oh1, cnt, 0.0), axis=-1, keepdims=True)
    r2 = jnp.sum(jnp.where(oh2, cnt, 0.0), axis=-1, keepdims=True)
    run[...] = run[...] + jnp.sum(both, axis=0, keepdims=True)
    cnt_ref[...] = run[...]

    info = jnp.zeros((tm, LANES), F32)
    for idx, val in enumerate((e1, e2, r1, r2, w1, w2)):
        info = jnp.where(lane == idx, val, info)
    info_ref[...] = info


def _router(h2d, g, router_pad):
    t, d = h2d.shape
    tm = TM_ROUTE
    const = lambda shape: pl.BlockSpec(shape, lambda i: (0, 0))
    return pl.pallas_call(
        _router_kernel,
        out_shape=(jax.ShapeDtypeStruct((t, LANES), F32), jax.ShapeDtypeStruct((1, LANES), F32)),
        grid=(t // tm,),
        in_specs=[pl.BlockSpec((tm, d), lambda i: (i, 0)), const((1, d)), const(router_pad.shape)],
        out_specs=(pl.BlockSpec((tm, LANES), lambda i: (i, 0)), const((1, LANES))),
        scratch_shapes=[pltpu.VMEM((1, LANES), F32)],
        compiler_params=pltpu.CompilerParams(dimension_semantics=("arbitrary",)),
        name="router",
    )(h2d, g.reshape(1, d), router_pad)


def _row_copy(src, src_row, dst, dst_row, sem):
    s = pl.ds(pl.multiple_of(src_row * SUBLANES, SUBLANES), SUBLANES)
    d = pl.ds(pl.multiple_of(dst_row * SUBLANES, SUBLANES), SUBLANES)
    return pltpu.make_async_copy(src.at[s], dst.at[d], sem)


def _store_tile_rows(ref, x):
    n = x.shape[0]
    for c in range(SUBLANES):
        ref[pl.ds(c, n, stride=SUBLANES), :] = x[:, c * LANES:(c + 1) * LANES]


def _load_tile_rows(ref):
    n = ref.shape[0] // SUBLANES
    return jnp.concatenate([ref[pl.ds(c, n, stride=SUBLANES), :] for c in range(SUBLANES)], axis=-1)


def _dispatch_kernel(slot_ref, end_ref, x_ref, g_ref, xs_hbm, hbuf, zbuf, sem, zsem):
    tm = x_ref.shape[0]
    ts = zbuf.shape[0] // SUBLANES
    i = pl.program_id(0)

    @pl.when(i == 0)
    def _():
        zbuf[...] = jnp.zeros_like(zbuf)

        def zero_tile(first_slot):
            rows = pl.ds(pl.multiple_of(first_slot * SUBLANES, SUBLANES), ts * SUBLANES)
            block = pltpu.make_async_copy(zbuf, xs_hbm.at[rows], zsem)
            block.start()
            block.wait()

        for e in range(N_EXPERTS):
            zero_tile(end_ref[e])
        n_tiles = xs_hbm.shape[0] // (ts * SUBLANES)
        for q in range(n_tiles - slot_ref.shape[0] // ts):
            tile = end_ref[N_EXPERTS] + q

            @pl.when(tile < n_tiles)
            def _():
                zero_tile(tile * ts)

    def drain(buf):
        def wait(r, c):
            for k in range(2):
                _row_copy(hbuf.at[buf], r, xs_hbm, 0, sem.at[buf]).wait()
            return c
        lax.fori_loop(0, tm, wait, 0, unroll=8)

    buf = i % 2

    @pl.when(i >= 2)
    def _():
        drain(buf)

    _store_tile_rows(hbuf.at[buf], _rms_rows(x_ref[...], g_ref[...]))

    def issue(r, c):
        for k in range(2):
            _row_copy(hbuf.at[buf], r, xs_hbm, slot_ref[2 * (i * tm + r) + k], sem.at[buf]).start(priority=k)
        return c
    lax.fori_loop(0, tm, issue, 0, unroll=8)

    n_steps = slot_ref.shape[0] // (2 * tm)

    @pl.when(i == n_steps - 1)
    def _():
        if n_steps > 1:
            drain(1 - buf)
        drain(buf)


def _dispatch(slots, ends, h2d, g, n_slots):
    t, d = h2d.shape
    assert d == SUBLANES * LANES, "a slot row must fill exactly one (8, 128) tile"
    tm = TM_ROUTE
    grid_spec = pltpu.PrefetchScalarGridSpec(
        num_scalar_prefetch=2,
        grid=(t // tm,),
        in_specs=[pl.BlockSpec((tm, d), lambda i, sl, cn: (i, 0)), pl.BlockSpec((1, d), lambda i, sl, cn: (0, 0))],
        out_specs=pl.BlockSpec(memory_space=pl.ANY),
        scratch_shapes=[pltpu.VMEM((2, tm * SUBLANES, LANES), F32), pltpu.VMEM((TS_MOE * SUBLANES, LANES), F32),
                        pltpu.SemaphoreType.DMA((2,)), pltpu.SemaphoreType.DMA])
    return pl.pallas_call(
        _dispatch_kernel,
        out_shape=jax.ShapeDtypeStruct((n_slots * SUBLANES, LANES), F32),
        grid_spec=grid_spec,
        compiler_params=pltpu.CompilerParams(dimension_semantics=("arbitrary",)),
        name="moe_dispatch",
    )(slots, ends, h2d, g.reshape(1, d))


def _moe_ffn_kernel(te_ref, used_ref, xs_ref, wg_hbm, wu_hbm, wd_hbm, y_ref, wg, wu, wd, stage, sem):
    i = pl.program_id(0)
    used = used_ref[i]
    valid = used > 0
    expert = te_ref[i]
    f = wg.shape[1]
    n_slot, rows, cols = stage.shape
    ts = xs_ref.shape[0] // SUBLANES

    pieces = [(src, dst, r0, c0, min(cols, dst.shape[1] - c0))
              for src, dst in ((wg_hbm, wg), (wu_hbm, wu), (wd_hbm, wd))
              for r0 in range(0, dst.shape[0], rows) for c0 in range(0, dst.shape[1], cols)]

    def copy(p):
        src, _, r0, c0, width = pieces[p]
        slot = p % n_slot
        return pltpu.make_async_copy(src.at[expert, pl.ds(r0, rows), pl.ds(c0, width)],
                                     stage.at[slot, :, pl.ds(0, width)], sem.at[slot])

    @pl.when(valid & ((i == 0) | (expert != te_ref[jnp.maximum(i - 1, 0)])))
    def _():
        for p in range(min(n_slot - 1, len(pieces))):
            copy(p).start()
        for p in range(len(pieces)):
            if p + n_slot - 1 < len(pieces):
                copy(p + n_slot - 1).start()
            copy(p).wait()
            _, dst, r0, c0, width = pieces[p]
            dst[r0:r0 + rows, c0:c0 + width] = stage[p % n_slot, :, 0:width].astype(BF16)

    for n in range(0, ts + 1, MOE_ROW_STEP):
        @pl.when((used > n - MOE_ROW_STEP) & (used <= n))
        def _():
            if n > 0:
                x = _load_tile_rows(xs_ref.at[pl.ds(0, n * SUBLANES)]).astype(BF16)
                out = None
                for lo in range(0, f, TF_MOE):
                    gte = _dot(x, wg[:, lo:lo + TF_MOE])
                    up = _dot(x, wu[:, lo:lo + TF_MOE])
                    act = (gte * jax.nn.sigmoid(gte) * up).astype(BF16)
                    part = _dot(act, wd[lo:lo + TF_MOE, :])
                    out = part if out is None else out + part
                _store_tile_rows(y_ref.at[pl.ds(0, n * SUBLANES)], out)
            if n < ts:
                y_ref[n * SUBLANES:, :] = jnp.zeros(((ts - n) * SUBLANES, LANES), F32)


def _moe_ffn(tile_expert, tile_used, xs, wg, wu, wd):
    n_exp, d, f = wg.shape
    ts = TS_MOE
    n_tiles = tile_expert.shape[0]
    hbm = pl.BlockSpec(memory_space=pl.ANY)
    tile_rows = pl.BlockSpec((ts * SUBLANES, LANES), lambda i, te, nvt: (i, 0))
    grid_spec = pltpu.PrefetchScalarGridSpec(
        num_scalar_prefetch=2,
        grid=(n_tiles,),
        in_specs=[tile_rows, hbm, hbm, hbm],
        out_specs=tile_rows,
        scratch_shapes=[pltpu.VMEM((d, f), BF16), pltpu.VMEM((d, f), BF16), pltpu.VMEM((f, d), BF16),
                        pltpu.VMEM((W_STAGE_SLOTS, W_STAGE_ROWS, W_STAGE_COLS), F32),
                        pltpu.SemaphoreType.DMA((W_STAGE_SLOTS,))])
    return pl.pallas_call(
        _moe_ffn_kernel,
        out_shape=jax.ShapeDtypeStruct((n_tiles * ts * SUBLANES, LANES), F32),
        grid_spec=grid_spec,
        compiler_params=pltpu.CompilerParams(dimension_semantics=("arbitrary",),
                                             vmem_limit_bytes=VMEM_LIMIT),
        name="moe_ffn",
    )(tile_expert, tile_used, xs, wg, wu, wd)


def _combine_kernel(slot_ref, h_ref, info_ref, y_hbm, o_ref, ybuf, sem):
    tm, d = h_ref.shape
    i = pl.program_id(0)
    n = pl.num_programs(0)

    def fetch(tile, start):
        buf = tile % 2

        def body(r, c):
            for k in range(2):
                cp = _row_copy(y_hbm, slot_ref[2 * (tile * tm + r) + k] if start else 0, ybuf.at[buf, k], r,
                               sem.at[buf])
                cp.start(priority=k) if start else cp.wait()
            return c
        lax.fori_loop(0, tm, body, 0, unroll=8)

    @pl.when(i == 0)
    def _():
        fetch(i, True)

    @pl.when(i + 1 < n)
    def _():
        fetch(i + 1, True)

    fetch(i, False)
    buf = i % 2
    w1 = info_ref[:, 4:5]
    w2 = info_ref[:, 5:6]
    o_ref[...] = h_ref[...] + (w1 * _load_tile_rows(ybuf.at[buf, 0]) + w2 * _load_tile_rows(ybuf.at[buf, 1]))


def _combine(slots, h2d, info, y):
    t, d = h2d.shape
    tm = TM_COMBINE
    grid_spec = pltpu.PrefetchScalarGridSpec(
        num_scalar_prefetch=1,
        grid=(t // tm,),
        in_specs=[pl.BlockSpec((tm, d), lambda i, sl: (i, 0)), pl.BlockSpec((tm, LANES), lambda i, sl: (i, 0)),
                  pl.BlockSpec(memory_space=pl.ANY)],
        out_specs=pl.BlockSpec((tm, d), lambda i, sl: (i, 0)),
        scratch_shapes=[pltpu.VMEM((2, 2, tm * SUBLANES, LANES), F32), pltpu.SemaphoreType.DMA((2,))])
    return pl.pallas_call(
        _combine_kernel,
        out_shape=jax.ShapeDtypeStruct((t, d), F32),
        grid_spec=grid_spec,
        compiler_params=pltpu.CompilerParams(dimension_semantics=("arbitrary",)),
        name="moe_combine",
    )(slots, h2d, info, y)


def _moe(h2d, g, router, wg, wu, wd):
    t, d = h2d.shape
    ts = TS_MOE
    router_pad = jnp.pad(router, ((0, 0), (0, LANES - N_EXPERTS)))
    info, counts = _router(h2d, g, router_pad)

    counts = counts[0, :N_EXPERTS].astype(jnp.int32)
    tiles_per_expert = (counts + ts - 1) // ts
    tile_end = jnp.cumsum(tiles_per_expert)
    starts = (tile_end - tiles_per_expert) * ts
    n_valid = tile_end[-1]
    chosen = info[:, 0:2].astype(jnp.int32)[..., None] == jnp.arange(N_EXPERTS, dtype=jnp.int32)
    slots = (jnp.sum(jnp.where(chosen, starts, 0), axis=-1) + info[:, 2:4].astype(jnp.int32)).reshape(-1)
    n_tiles = 2 * t // ts + N_EXPERTS
    tile_id = jnp.arange(n_tiles, dtype=jnp.int32)
    tile = jnp.minimum(tile_id, n_valid - 1)
    tile_expert = jnp.sum((tile[:, None] >= tile_end[None, :]).astype(jnp.int32), axis=1)
    is_expert = tile_expert[:, None] == jnp.arange(N_EXPERTS, dtype=jnp.int32)
    tokens_left = jnp.sum(jnp.where(is_expert, starts + counts, 0), axis=1) - tile * ts
    tile_used = jnp.where(tile_id < n_valid, jnp.clip(tokens_left, 0, ts), 0)
    ends = jnp.concatenate([starts + counts, n_valid.reshape(1)])

    xs = _dispatch(slots, ends, h2d, g, (n_tiles + 1) * ts)
    y = _moe_ffn(tile_expert, tile_used, xs, wg, wu, wd)
    return _combine(slots, h2d, info, y)


def _regroup_heads(w, axis):
    shape = w.shape
    split = shape[:axis] + (SWA_KV_HEADS, SWA_GROUP, HEAD_DIM) + shape[axis + 1:]
    return jnp.swapaxes(w.reshape(split), axis, axis + 1).reshape(shape)


def kernel(x, mem, mem_norm_g, w_mem_kv, mem_k_norm_g, cv_attn_norm_g, cv_w_in, cv_b_glu, cv_dw_w, cv_dw_b, cv_ln_g, cv_ln_b, cv_memq_norm_g, cv_w_out, cv_ffn_norm_g, cv_w_gate, cv_w_up, cv_w_down, sw_attn_norm_g, sw_w_in, sw_q_norm_g, sw_k_norm_g, sw_sinks, sw_memq_norm_g, sw_w_out, sw_ffn_norm_g, sw_router, sw_we_gate, sw_we_up, sw_we_down):
    b, s, d = x.shape
    scale = HEAD_DIM ** -0.5
    lane_head = jnp.arange(MEM_WIDTH) // HEAD_DIM
    head_ones = (lane_head[:, None] == lane_head[None, :]).astype(BF16)
    tile4 = lambda gvec: jnp.tile(gvec, MEM_HEADS).reshape(1, MEM_WIDTH)

    kexp, vexp = _mem_kv(mem, mem_norm_g, w_mem_kv.astype(BF16), tile4(mem_k_norm_g), head_ones)

    h = x
    depth = cv_w_in.shape[0] + sw_w_in.shape[0]
    for i in range(depth):
        j = i // 2
        if i % 2 == 0:
            h = _conv_mixer(h, cv_attn_norm_g[j], cv_w_in[j].astype(BF16), cv_b_glu[j], cv_dw_w[j], cv_dw_b[j],
                            cv_ln_g[j], cv_ln_b[j], tile4(cv_memq_norm_g[j]) * scale, head_ones, kexp, vexp,
                            cv_w_out[j].astype(BF16))
            h = _dense_ffn(h.reshape(b * s, d), cv_ffn_norm_g[j], cv_w_gate[j].astype(BF16),
                           cv_w_up[j].astype(BF16), cv_w_down[j].astype(BF16)).reshape(b, s, d)
        else:
            qw = SWA_KV_HEADS * SWA_GROUP * HEAD_DIM
            w_in = jnp.concatenate([_regroup_heads(sw_w_in[j][:, :qw], 1), sw_w_in[j][:, qw:]], axis=1)
            w_out = jnp.concatenate([_regroup_heads(sw_w_out[j][:qw], 0), sw_w_out[j][qw:]], axis=0)
            h = _swa_mixer(h, sw_attn_norm_g[j], w_in.astype(BF16), tile4(sw_q_norm_g[j]) * scale,
                           tile4(sw_k_norm_g[j]), sw_sinks[j], tile4(sw_memq_norm_g[j]) * scale, head_ones,
                           kexp, vexp, w_out.astype(BF16))
            h = _moe(h.reshape(b * s, d), sw_ffn_norm_g[j], sw_router[j], sw_we_gate[j], sw_we_up[j],
                     sw_we_down[j]).reshape(b, s, d)
    return h
```

```python
import jax
import jax.numpy as jnp
from jax import lax
from jax.experimental import pallas as pl
from jax.experimental.pallas import tpu as pltpu

F32 = jnp.float32
BF16 = jnp.bfloat16

HEAD_DIM = 64
MEM_HEADS = 4
MEM_WIDTH = MEM_HEADS * HEAD_DIM
CONV_WIDTH = 31
CONV_HALO = 32
SWA_KV_HEADS = 4
SWA_GROUP = 3
SWA_BLOCK = 128
N_EXPERTS = 8
RMS_EPS = 1e-6
LN_EPS = 1e-5
ALIBI_MAX_BIAS = 8.0
NEG_INF = -1e30
LANES = 128
SUBLANES = 8
VMEM_LIMIT = 56 * 1024 * 1024

TM_MIX = 512
TM_SWA = 1024
TM_FFN = 512
FFN_SPLIT = 1536
TM_ROUTE = 1024
TM_COMBINE = 512
MOE_ROW_STEP = 128
TS_MOE = 512
TF_MOE = 1792
W_CHUNK = 512
W_PIECE_BYTES = 1024 * 1024


def _dot(a, b):
    return jnp.dot(a, b, preferred_element_type=F32)


def _dot_nt(a, b):
    return lax.dot_general(a, b, (((1,), (1,)), ((), ())), preferred_element_type=F32)


def _rms_rows(x, g):
    ms = jnp.mean(x * x, axis=-1, keepdims=True)
    return x * lax.rsqrt(ms + RMS_EPS) * g


def _split_bf16(x):
    hi = x.astype(BF16)
    return hi, (x - hi.astype(F32)).astype(BF16)


def _head_mean_sq(x, head_ones):
    hi, lo = _split_bf16(x * x)
    return (_dot(hi, head_ones) + _dot(lo, head_ones)) * (1.0 / HEAD_DIM)


def _head_rms(x, head_ones, g):
    return x * lax.rsqrt(_head_mean_sq(x, head_ones) + RMS_EPS) * g


def _head_lane_mask(h):
    lane = lax.broadcasted_iota(jnp.int32, (1, MEM_WIDTH), 1)
    return ((lane >= h * HEAD_DIM) & (lane < (h + 1) * HEAD_DIM)).astype(F32)


def _mem_attention(q_mem, head_ones, qg_scaled, kexp, vexp):
    qn = _head_rms(q_mem, head_ones, qg_scaled).astype(BF16)
    s = _dot_nt(qn, kexp)
    m_len = kexp.shape[0] // MEM_HEADS
    ps = []
    for h in range(MEM_HEADS):
        sh = s[:, h * m_len:(h + 1) * m_len]
        e = jnp.exp(sh - jnp.max(sh, axis=-1, keepdims=True))
        ps.append((e * (1.0 / jnp.sum(e, axis=-1, keepdims=True))).astype(BF16))
    return _dot(jnp.concatenate(ps, axis=-1), vexp)


def _mem_kv_kernel(mem_ref, g_ref, w_ref, kg_ref, ones_ref, kexp_ref, vexp_ref):
    x = mem_ref[0]
    hn = _rms_rows(x, g_ref[...]).astype(BF16)
    kv = _dot(hn, w_ref[...])
    k = _head_rms(kv[:, :MEM_WIDTH], ones_ref[...], kg_ref[...])
    v = kv[:, MEM_WIDTH:]
    m_len = x.shape[0]
    for h in range(MEM_HEADS):
        mask = _head_lane_mask(h)
        kexp_ref[0, h * m_len:(h + 1) * m_len, :] = (k * mask).astype(BF16)
        vexp_ref[0, h * m_len:(h + 1) * m_len, :] = (v * mask).astype(BF16)


def _mem_kv(mem, g, w_bf, kg_tiled, head_ones):
    b, m_len, d = mem.shape
    out = jax.ShapeDtypeStruct((b, MEM_HEADS * m_len, MEM_WIDTH), BF16)
    const = lambda shape: pl.BlockSpec(shape, lambda i: (0,) * len(shape))
    return pl.pallas_call(
        _mem_kv_kernel,
        out_shape=(out, out),
        grid=(b,),
        in_specs=[pl.BlockSpec((1, m_len, d), lambda i: (i, 0, 0)),
                  const((1, d)), const(w_bf.shape), const((1, MEM_WIDTH)), const(head_ones.shape)],
        out_specs=(pl.BlockSpec((1, MEM_HEADS * m_len, MEM_WIDTH), lambda i: (i, 0, 0)),) * 2,
        name="mem_kv",
    )(mem, g.reshape(1, d), w_bf, kg_tiled, head_ones)


def _conv_mixer_kernel(x_ref, g_ref, win_ref, bglu_ref, dww_ref, dwb_ref, lng_ref, lnb_ref,
                       mqg_ref, ones_ref, kexp_ref, vexp_ref, wout_ref, o_ref, ubuf, cbuf):
    tm = x_ref.shape[1]
    cc = dww_ref.shape[2]

    @pl.when(pl.program_id(1) == 0)
    def _():
        ubuf[...] = jnp.zeros_like(ubuf)

    @pl.when(pl.program_id(1) > 0)
    def _():
        ubuf[:, 0:CONV_HALO, :] = ubuf[:, tm:tm + CONV_HALO, :]

    x = x_ref[0]
    hn = _rms_rows(x, g_ref[...]).astype(BF16)
    proj = _dot(hn, win_ref[...])
    a = proj[:, :cc] + bglu_ref[:, :cc]
    gate = proj[:, cc:2 * cc] + bglu_ref[:, cc:]
    u = a * jax.nn.sigmoid(gate)
    for r in range(SUBLANES):
        ubuf[r, CONV_HALO - r:CONV_HALO - r + tm, :] = u

    rows = 32
    groups = rows // SUBLANES
    base = CONV_HALO - (CONV_WIDTH - 1)
    for r0 in range(0, tm, rows):
        acc = jnp.broadcast_to(dwb_ref[...][None], (groups, SUBLANES, cc))
        for k in range(CONV_WIDTH):
            r = (base + k) % SUBLANES
            lo = r0 + base + k - r
            acc = acc + ubuf[r, lo:lo + rows, :].reshape(groups, SUBLANES, cc) * dww_ref[k][None]
        mu = jnp.mean(acc, axis=-1, keepdims=True)
        dlt = acc - mu
        var = jnp.mean(dlt * dlt, axis=-1, keepdims=True)
        y = dlt * lax.rsqrt(var + LN_EPS) * lng_ref[...][None] + lnb_ref[...][None]
        cbuf[r0:r0 + rows, :] = (y * jax.nn.sigmoid(y)).reshape(rows, cc).astype(BF16)

    m = _mem_attention(proj[:, 2 * cc:], ones_ref[...], mqg_ref[...], kexp_ref[0], vexp_ref[0])
    out = _dot(cbuf[...], wout_ref[0:cc, :]) + _dot(m.astype(BF16), wout_ref[cc:, :])
    o_ref[0] = x + out


def _conv_mixer(h, g, win_bf, bglu, dww, dwb, lng, lnb, mqg_scaled, head_ones, kexp, vexp, wout_bf):
    b, s, d = h.shape
    cc = dww.shape[1]
    tm = TM_MIX
    const = lambda shape: pl.BlockSpec(shape, lambda i, j: (0,) * len(shape))
    per_batch = lambda arr: pl.BlockSpec((1,) + arr.shape[1:], lambda i, j: (i, 0, 0))
    rep = lambda v: jnp.broadcast_to(v[..., None, :], v.shape[:-1] + (SUBLANES, cc))
    return pl.pallas_call(
        _conv_mixer_kernel,
        out_shape=jax.ShapeDtypeStruct(h.shape, F32),
        grid=(b, s // tm),
        in_specs=[pl.BlockSpec((1, tm, d), lambda i, j: (i, j, 0)),
                  const((1, d)), const(win_bf.shape), const((1, 2 * cc)), const((dww.shape[0], SUBLANES, cc)),
                  const((SUBLANES, cc)), const((SUBLANES, cc)), const((SUBLANES, cc)), const((1, MEM_WIDTH)),
                  const(head_ones.shape), per_batch(kexp), per_batch(vexp), const(wout_bf.shape)],
        out_specs=pl.BlockSpec((1, tm, d), lambda i, j: (i, j, 0)),
        scratch_shapes=[pltpu.VMEM((SUBLANES, CONV_HALO + tm, cc), F32), pltpu.VMEM((tm, cc), BF16)],
        compiler_params=pltpu.CompilerParams(dimension_semantics=("arbitrary", "arbitrary"),
                                             vmem_limit_bytes=VMEM_LIMIT),
        name="conv_mixer",
    )(h, g.reshape(1, d), win_bf, bglu.reshape(1, -1), rep(dww), rep(dwb), rep(lng), rep(lnb), mqg_scaled,
      head_ones, kexp, vexp, wout_bf)


def _dense_ffn_kernel(x_ref, g_ref, wg_ref, wu_ref, wd_ref, o_ref):
    x = x_ref[...]
    hn = _rms_rows(x, g_ref[...]).astype(BF16)
    f = wg_ref.shape[1]
    out = x
    for lo, hi in ((0, FFN_SPLIT), (FFN_SPLIT, f)):
        gte = _dot(hn, wg_ref[:, lo:hi])
        up = _dot(hn, wu_ref[:, lo:hi])
        act = (gte * jax.nn.sigmoid(gte) * up).astype(BF16)
        out = out + _dot(act, wd_ref[lo:hi, :])
    o_ref[...] = out


def _dense_ffn(h2d, g, wg_bf, wu_bf, wd_bf):
    t, d = h2d.shape
    tm = TM_FFN
    resident = lambda shape: pl.BlockSpec(shape, lambda i: (0, 0), pipeline_mode=pl.Buffered(1))
    return pl.pallas_call(
        _dense_ffn_kernel,
        out_shape=jax.ShapeDtypeStruct((t, d), F32),
        grid=(t // tm,),
        in_specs=[pl.BlockSpec((tm, d), lambda i: (i, 0)), resident((1, d)),
                  resident(wg_bf.shape), resident(wu_bf.shape), resident(wd_bf.shape)],
        out_specs=pl.BlockSpec((tm, d), lambda i: (i, 0)),
        compiler_params=pltpu.CompilerParams(dimension_semantics=("arbitrary",),
                                             vmem_limit_bytes=VMEM_LIMIT),
        name="dense_ffn",
    )(h2d, g.reshape(1, d), wg_bf, wu_bf, wd_bf)


def _swa_bias_tables():
    blk = SWA_BLOCK
    n_heads = SWA_KV_HEADS * SWA_GROUP
    slopes = jnp.exp2(-ALIBI_MAX_BIAS * (jnp.arange(n_heads, dtype=F32) + 1.0) / n_heads)
    dist = (jnp.arange(blk)[:, None] + blk - jnp.arange(2 * blk)[None, :]).astype(F32)
    window = (dist >= 0) & (dist < blk)
    bias = -slopes[:, None, None] * dist[None]
    current = jnp.arange(2 * blk)[None, :] >= blk
    return jnp.stack([jnp.where(window[None], bias, NEG_INF),
                      jnp.where((window & current)[None], bias, NEG_INF)])


def _swa_mixer_kernel(sinks_ref, x_ref, g_ref, win_ref, qg_ref, kg_ref, mqg_ref, ones_ref, bias_ref,
                      kexp_ref, vexp_ref, wout_ref, o_ref, kbuf, vbuf, obuf):
    tm = x_ref.shape[1]
    blk = SWA_BLOCK
    kvw = SWA_KV_HEADS * HEAD_DIM
    qw = SWA_GROUP * kvw
    first_tile = pl.program_id(1) == 0

    @pl.when(first_tile)
    def _():
        kbuf[:, 0:blk, :] = jnp.zeros((SWA_KV_HEADS, blk, kvw), BF16)
        vbuf[:, 0:blk, :] = jnp.zeros((SWA_KV_HEADS, blk, kvw), BF16)

    x = x_ref[0]
    hn = _rms_rows(x, g_ref[...]).astype(BF16)
    proj = _dot(hn, win_ref[...])
    ones = ones_ref[...]
    kn = _head_rms(proj[:, qw:qw + kvw], ones, kg_ref[...])
    v = proj[:, qw + kvw:qw + 2 * kvw]
    for h in range(SWA_KV_HEADS):
        mask = _head_lane_mask(h)
        kbuf[h, blk:blk + tm, :] = (kn * mask).astype(BF16)
        vbuf[h, blk:blk + tm, :] = (v * mask).astype(BF16)

    qn = [_head_rms(proj[:, g * kvw:(g + 1) * kvw], ones, qg_ref[...]).astype(BF16) for g in range(SWA_GROUP)]
    first_block = jnp.where(first_tile, 1, 0)
    for n in range(tm // blk):
        rows = slice(n * blk, (n + 1) * blk)
        keys = slice(n * blk, (n + 2) * blk)
        table = first_block if n == 0 else 0
        qs = jnp.concatenate([q[rows, :] for q in qn], axis=0)
        og = None
        for h in range(SWA_KV_HEADS):
            s = _dot_nt(qs, kbuf[h, keys, :])
            ps = []
            for g in range(SWA_GROUP):
                head = h * SWA_GROUP + g
                sg = s[g * blk:(g + 1) * blk, :] + bias_ref[table, head]
                sink = sinks_ref[head]
                mx = jnp.maximum(jnp.max(sg, axis=-1, keepdims=True), sink)
                e = jnp.exp(sg - mx)
                den = jnp.sum(e, axis=-1, keepdims=True) + jnp.exp(sink - mx)
                ps.append((e * (1.0 / den)).astype(BF16))
            part = _dot(jnp.concatenate(ps, axis=0), vbuf[h, keys, :])
            og = part if og is None else og + part
        for g in range(SWA_GROUP):
            obuf[rows, g * kvw:(g + 1) * kvw] = og[g * blk:(g + 1) * blk, :].astype(BF16)

    kbuf[:, 0:blk, :] = kbuf[:, tm:tm + blk, :]
    vbuf[:, 0:blk, :] = vbuf[:, tm:tm + blk, :]

    m = _mem_attention(proj[:, qw + 2 * kvw:], ones, mqg_ref[...], kexp_ref[0], vexp_ref[0])
    out = _dot(obuf[...], wout_ref[0:qw, :]) + _dot(m.astype(BF16), wout_ref[qw:, :])
    o_ref[0] = x + out


def _swa_mixer(h, g, win_bf, qg_scaled, kg_tiled, sinks, mqg_scaled, head_ones, kexp, vexp, wout_bf):
    b, s, d = h.shape
    tm = TM_SWA
    kvw = SWA_KV_HEADS * HEAD_DIM
    bias = _swa_bias_tables()
    const = lambda shape: pl.BlockSpec(shape, lambda i, j, sk: (0,) * len(shape))
    per_batch = lambda arr: pl.BlockSpec((1,) + arr.shape[1:], lambda i, j, sk: (i, 0, 0))
    grid_spec = pltpu.PrefetchScalarGridSpec(
        num_scalar_prefetch=1,
        grid=(b, s // tm),
        in_specs=[pl.BlockSpec((1, tm, d), lambda i, j, sk: (i, j, 0)),
                  const((1, d)), const(win_bf.shape), const((1, kvw)), const((1, kvw)),
                  const((1, MEM_WIDTH)), const(head_ones.shape), const(bias.shape), per_batch(kexp),
                  per_batch(vexp), const(wout_bf.shape)],
        out_specs=pl.BlockSpec((1, tm, d), lambda i, j, sk: (i, j, 0)),
        scratch_shapes=[pltpu.VMEM((SWA_KV_HEADS, SWA_BLOCK + tm, kvw), BF16),
                        pltpu.VMEM((SWA_KV_HEADS, SWA_BLOCK + tm, kvw), BF16),
                        pltpu.VMEM((tm, SWA_GROUP * kvw), BF16)])
    return pl.pallas_call(
        _swa_mixer_kernel,
        out_shape=jax.ShapeDtypeStruct(h.shape, F32),
        grid_spec=grid_spec,
        compiler_params=pltpu.CompilerParams(dimension_semantics=("arbitrary", "arbitrary"),
                                             vmem_limit_bytes=VMEM_LIMIT),
        name="swa_mixer",
    )(sinks, h, g.reshape(1, d), win_bf, qg_scaled, kg_tiled, mqg_scaled, head_ones, bias, kexp, vexp, wout_bf)


def _router_kernel(x_ref, g_ref, r_ref, info_ref, cnt_ref, run):
    tm = x_ref.shape[0]

    @pl.when(pl.program_id(0) == 0)
    def _():
        run[...] = jnp.zeros_like(run)

    hn_hi, hn_lo = _split_bf16(_rms_rows(x_ref[...], g_ref[...]))
    r_hi, r_lo = _split_bf16(r_ref[...])
    logits = _dot(hn_hi, r_hi) + (_dot(hn_hi, r_lo) + _dot(hn_lo, r_hi))
    lane = lax.broadcasted_iota(jnp.int32, (tm, LANES), 1).astype(F32)
    logits = jnp.where(lane < N_EXPERTS, logits, -jnp.inf)
    m1 = jnp.max(logits, axis=-1, keepdims=True)
    e1 = jnp.min(jnp.where(logits == m1, lane, float(LANES)), axis=-1, keepdims=True)
    oh1 = lane == e1
    rest = jnp.where(oh1, -jnp.inf, logits)
    m2 = jnp.max(rest, axis=-1, keepdims=True)
    e2 = jnp.min(jnp.where(rest == m2, lane, float(LANES)), axis=-1, keepdims=True)
    oh2 = lane == e2
    z = jnp.exp(m2 - m1)
    w1 = 1.0 / (1.0 + z)
    w2 = z / (1.0 + z)

    both = jnp.where(oh1 | oh2, 1.0, 0.0)
    ri = lax.broadcasted_iota(jnp.int32, (tm, tm), 0)
    ci = lax.broadcasted_iota(jnp.int32, (tm, tm), 1)
    before = jnp.where(ci < ri, 1.0, 0.0).astype(BF16)
    cnt = _dot(before, both.astype(BF16)) + run[...]
    r1 = jnp.sum(jnp.where(oh1, cnt, 0.0), axis=-1, keepdims=True)
    r2 = jnp.sum(jnp.where(oh2, cnt, 0.0), axis=-1, keepdims=True)
    run[...] = run[...] + jnp.sum(both, axis=0, keepdims=True)
    cnt_ref[...] = run[...]

    info = jnp.zeros((tm, LANES), F32)
    for idx, val in enumerate((e1, e2, r1, r2, w1, w2)):
        info = jnp.where(lane == idx, val, info)
    info_ref[...] = info


def _router(h2d, g, router_pad):
    t, d = h2d.shape
    tm = TM_ROUTE
    const = lambda shape: pl.BlockSpec(shape, lambda i: (0, 0))
    return pl.pallas_call(
        _router_kernel,
        out_shape=(jax.ShapeDtypeStruct((t, LANES), F32), jax.ShapeDtypeStruct((1, LANES), F32)),
        grid=(t // tm,),
        in_specs=[pl.BlockSpec((tm, d), lambda i: (i, 0)), const((1, d)), const(router_pad.shape)],
        out_specs=(pl.BlockSpec((tm, LANES), lambda i: (i, 0)), const((1, LANES))),
        scratch_shapes=[pltpu.VMEM((1, LANES), F32)],
        compiler_params=pltpu.CompilerParams(dimension_semantics=("arbitrary",)),
        name="router",
    )(h2d, g.reshape(1, d), router_pad)


def _row_copy(src, src_row, dst, dst_row, sem):
    s = pl.ds(pl.multiple_of(src_row * SUBLANES, SUBLANES), SUBLANES)
    d = pl.ds(pl.multiple_of(dst_row * SUBLANES, SUBLANES), SUBLANES)
    return pltpu.make_async_copy(src.at[s], dst.at[d], sem)


def _store_tile_rows(ref, x):
    n = x.shape[0]
    for c in range(SUBLANES):
        ref[pl.ds(c, n, stride=SUBLANES), :] = x[:, c * LANES:(c + 1) * LANES]


def _load_tile_rows(ref):
    n = ref.shape[0] // SUBLANES
    return jnp.concatenate([ref[pl.ds(c, n, stride=SUBLANES), :] for c in range(SUBLANES)], axis=-1)


def _dispatch_kernel(slot_ref, end_ref, x_ref, g_ref, xs_hbm, hbuf, zbuf, sem, zsem):
    tm = x_ref.shape[0]
    ts = zbuf.shape[0] // SUBLANES
    i = pl.program_id(0)

    @pl.when(i == 0)
    def _():
        zbuf[...] = jnp.zeros_like(zbuf)

        def zero_tile(first_slot):
            rows = pl.ds(pl.multiple_of(first_slot * SUBLANES, SUBLANES), ts * SUBLANES)
            block = pltpu.make_async_copy(zbuf, xs_hbm.at[rows], zsem)
            block.start()
            block.wait()

        for e in range(N_EXPERTS):
            zero_tile(end_ref[e])
        n_tiles = xs_hbm.shape[0] // (ts * SUBLANES)
        for q in range(n_tiles - slot_ref.shape[0] // ts):
            tile = end_ref[N_EXPERTS] + q

            @pl.when(tile < n_tiles)
            def _():
                zero_tile(tile * ts)

    def drain(buf):
        def wait(r, c):
            for k in range(2):
                _row_copy(hbuf.at[buf], r, xs_hbm, 0, sem.at[buf]).wait()
            return c
        lax.fori_loop(0, tm, wait, 0, unroll=8)

    buf = i % 2

    @pl.when(i >= 2)
    def _():
        drain(buf)

    _store_tile_rows(hbuf.at[buf], _rms_rows(x_ref[...], g_ref[...]))

    def issue(r, c):
        for k in range(2):
            _row_copy(hbuf.at[buf], r, xs_hbm, slot_ref[2 * (i * tm + r) + k], sem.at[buf]).start(priority=k)
        return c
    lax.fori_loop(0, tm, issue, 0, unroll=8)

    n_steps = slot_ref.shape[0] // (2 * tm)

    @pl.when(i == n_steps - 1)
    def _():
        if n_steps > 1:
            drain(1 - buf)
        drain(buf)


def _dispatch(slots, ends, h2d, g, n_slots):
    t, d = h2d.shape
    assert d == SUBLANES * LANES, "a slot row must fill exactly one (8, 128) tile"
    tm = TM_ROUTE
    grid_spec = pltpu.PrefetchScalarGridSpec(
        num_scalar_prefetch=2,
        grid=(t // tm,),
        in_specs=[pl.BlockSpec((tm, d), lambda i, sl, cn: (i, 0)), pl.BlockSpec((1, d), lambda i, sl, cn: (0, 0))],
        out_specs=pl.BlockSpec(memory_space=pl.ANY),
        scratch_shapes=[pltpu.VMEM((2, tm * SUBLANES, LANES), F32), pltpu.VMEM((TS_MOE * SUBLANES, LANES), F32),
                        pltpu.SemaphoreType.DMA((2,)), pltpu.SemaphoreType.DMA])
    return pl.pallas_call(
        _dispatch_kernel,
        out_shape=jax.ShapeDtypeStruct((n_slots * SUBLANES, LANES), F32),
        grid_spec=grid_spec,
        compiler_params=pltpu.CompilerParams(dimension_semantics=("arbitrary",)),
        name="moe_dispatch",
    )(slots, ends, h2d, g.reshape(1, d))


def _moe_ffn_kernel(te_ref, used_ref, xs_ref, wg_hbm, wu_hbm, wd_hbm, y_ref, wg, wu, wd, stage_in, stage_out, sem):
    i = pl.program_id(0)
    used = used_ref[i]
    expert = te_ref[i]
    f = wg.shape[1]
    ts = xs_ref.shape[0] // SUBLANES
    in_rows, chunk = stage_in.shape[1:]
    out_rows = stage_out.shape[1]
    first = (used > 0) & ((i == 0) | (expert != te_ref[jnp.maximum(i - 1, 0)]))

    def chunk_pieces(c):
        out = []
        cols = pl.ds(c * chunk, chunk)
        for m, (src, dst) in enumerate(((wg_hbm, wg), (wu_hbm, wu))):
            for r in range(wg.shape[0] // in_rows):
                slot = m * (wg.shape[0] // in_rows) + r
                rows = pl.ds(r * in_rows, in_rows)
                out.append((pltpu.make_async_copy(src.at[expert, rows, cols], stage_in.at[slot], sem.at[slot]),
                            stage_in.at[slot], dst.at[rows, cols]))
        for r in range(chunk // out_rows):
            slot = stage_in.shape[0] + r
            rows = pl.ds(c * chunk + r * out_rows, out_rows)
            out.append((pltpu.make_async_copy(wd_hbm.at[expert, rows, :], stage_out.at[r], sem.at[slot]),
                        stage_out.at[r], wd.at[rows, :]))
        return out

    def ffn(x, lo, width):
        gte = _dot(x, wg[:, lo:lo + width])
        up = _dot(x, wu[:, lo:lo + width])
        act = (gte * jax.nn.sigmoid(gte) * up).astype(BF16)
        return _dot(act, wd[lo:lo + width, :])

    @pl.when(first)
    def _():
        x = _load_tile_rows(xs_ref).astype(BF16)
        for copy, _, _ in chunk_pieces(0):
            copy.start()
        out = None
        for c in range(f // chunk):
            for copy, stage, dst in chunk_pieces(c):
                copy.wait()
                dst[...] = stage[...].astype(BF16)
            if c + 1 < f // chunk:
                for copy, _, _ in chunk_pieces(c + 1):
                    copy.start()
            part = ffn(x, c * chunk, chunk)
            out = part if out is None else out + part
        _store_tile_rows(y_ref, out)

    for n in range(0, ts + 1, MOE_ROW_STEP):
        @pl.when(jnp.logical_not(first) & (used > n - MOE_ROW_STEP) & (used <= n))
        def _():
            if n > 0:
                x = _load_tile_rows(xs_ref.at[pl.ds(0, n * SUBLANES)]).astype(BF16)
                out = None
                for lo in range(0, f, TF_MOE):
                    part = ffn(x, lo, TF_MOE)
                    out = part if out is None else out + part
                _store_tile_rows(y_ref.at[pl.ds(0, n * SUBLANES)], out)
            if n < ts:
                y_ref[n * SUBLANES:, :] = jnp.zeros(((ts - n) * SUBLANES, LANES), F32)


def _moe_ffn(tile_expert, tile_used, xs, wg, wu, wd):
    n_exp, d, f = wg.shape
    ts = TS_MOE
    n_tiles = tile_expert.shape[0]
    assert f % W_CHUNK == 0 and f % TF_MOE == 0
    n_in = d * W_CHUNK * 4 // W_PIECE_BYTES
    n_out = W_CHUNK * d * 4 // W_PIECE_BYTES
    hbm = pl.BlockSpec(memory_space=pl.ANY)
    tile_rows = pl.BlockSpec((ts * SUBLANES, LANES), lambda i, te, nvt: (i, 0))
    grid_spec = pltpu.PrefetchScalarGridSpec(
        num_scalar_prefetch=2,
        grid=(n_tiles,),
        in_specs=[tile_rows, hbm, hbm, hbm],
        out_specs=tile_rows,
        scratch_shapes=[pltpu.VMEM((d, f), BF16), pltpu.VMEM((d, f), BF16), pltpu.VMEM((f, d), BF16),
                        pltpu.VMEM((2 * n_in, d // n_in, W_CHUNK), F32),
                        pltpu.VMEM((n_out, W_CHUNK // n_out, d), F32),
                        pltpu.SemaphoreType.DMA((2 * n_in + n_out,))])
    return pl.pallas_call(
        _moe_ffn_kernel,
        out_shape=jax.ShapeDtypeStruct((n_tiles * ts * SUBLANES, LANES), F32),
        grid_spec=grid_spec,
        compiler_params=pltpu.CompilerParams(dimension_semantics=("arbitrary",),
                                             vmem_limit_bytes=VMEM_LIMIT),
        name="moe_ffn",
    )(tile_expert, tile_used, xs, wg, wu, wd)


def _combine_kernel(slot_ref, h_ref, info_ref, y_hbm, o_ref, ybuf, sem):
    tm, d = h_ref.shape
    i = pl.program_id(0)
    n = pl.num_programs(0)

    def fetch(tile, start):
        buf = tile % 2

        def body(r, c):
            for k in range(2):
                cp = _row_copy(y_hbm, slot_ref[2 * (tile * tm + r) + k] if start else 0, ybuf.at[buf, k], r,
                               sem.at[buf])
                cp.start(priority=k) if start else cp.wait()
            return c
        lax.fori_loop(0, tm, body, 0, unroll=8)

    @pl.when(i == 0)
    def _():
        fetch(i, True)

    @pl.when(i + 1 < n)
    def _():
        fetch(i + 1, True)

    fetch(i, False)
    buf = i % 2
    w1 = info_ref[:, 4:5]
    w2 = info_ref[:, 5:6]
    o_ref[...] = h_ref[...] + (w1 * _load_tile_rows(ybuf.at[buf, 0]) + w2 * _load_tile_rows(ybuf.at[buf, 1]))


def _combine(slots, h2d, info, y):
    t, d = h2d.shape
    tm = TM_COMBINE
    grid_spec = pltpu.PrefetchScalarGridSpec(
        num_scalar_prefetch=1,
        grid=(t // tm,),
        in_specs=[pl.BlockSpec((tm, d), lambda i, sl: (i, 0)), pl.BlockSpec((tm, LANES), lambda i, sl: (i, 0)),
                  pl.BlockSpec(memory_space=pl.ANY)],
        out_specs=pl.BlockSpec((tm, d), lambda i, sl: (i, 0)),
        scratch_shapes=[pltpu.VMEM((2, 2, tm * SUBLANES, LANES), F32), pltpu.SemaphoreType.DMA((2,))])
    return pl.pallas_call(
        _combine_kernel,
        out_shape=jax.ShapeDtypeStruct((t, d), F32),
        grid_spec=grid_spec,
        compiler_params=pltpu.CompilerParams(dimension_semantics=("arbitrary",)),
        name="moe_combine",
    )(slots, h2d, info, y)


def _moe(h2d, g, router, wg, wu, wd):
    t, d = h2d.shape
    ts = TS_MOE
    router_pad = jnp.pad(router, ((0, 0), (0, LANES - N_EXPERTS)))
    info, counts = _router(h2d, g, router_pad)

    counts = counts[0, :N_EXPERTS].astype(jnp.int32)
    tiles_per_expert = (counts + ts - 1) // ts
    tile_end = jnp.cumsum(tiles_per_expert)
    starts = (tile_end - tiles_per_expert) * ts
    n_valid = tile_end[-1]
    chosen = info[:, 0:2].astype(jnp.int32)[..., None] == jnp.arange(N_EXPERTS, dtype=jnp.int32)
    slots = (jnp.sum(jnp.where(chosen, starts, 0), axis=-1) + info[:, 2:4].astype(jnp.int32)).reshape(-1)
    n_tiles = 2 * t // ts + N_EXPERTS
    tile_id = jnp.arange(n_tiles, dtype=jnp.int32)
    tile = jnp.minimum(tile_id, n_valid - 1)
    tile_expert = jnp.sum((tile[:, None] >= tile_end[None, :]).astype(jnp.int32), axis=1)
    is_expert = tile_expert[:, None] == jnp.arange(N_EXPERTS, dtype=jnp.int32)
    tokens_left = jnp.sum(jnp.where(is_expert, starts + counts, 0), axis=1) - tile * ts
    tile_used = jnp.where(tile_id < n_valid, jnp.clip(tokens_left, 0, ts), 0)
    ends = jnp.concatenate([starts + counts, n_valid.reshape(1)])

    xs = _dispatch(slots, ends, h2d, g, (n_tiles + 1) * ts)
    y = _moe_ffn(tile_expert, tile_used, xs, wg, wu, wd)
    return _combine(slots, h2d, info, y)


def _regroup_heads(w, axis):
    shape = w.shape
    split = shape[:axis] + (SWA_KV_HEADS, SWA_GROUP, HEAD_DIM) + shape[axis + 1:]
    return jnp.swapaxes(w.reshape(split), axis, axis + 1).reshape(shape)


def kernel(x, mem, mem_norm_g, w_mem_kv, mem_k_norm_g, cv_attn_norm_g, cv_w_in, cv_b_glu, cv_dw_w, cv_dw_b, cv_ln_g, cv_ln_b, cv_memq_norm_g, cv_w_out, cv_ffn_norm_g, cv_w_gate, cv_w_up, cv_w_down, sw_attn_norm_g, sw_w_in, sw_q_norm_g, sw_k_norm_g, sw_sinks, sw_memq_norm_g, sw_w_out, sw_ffn_norm_g, sw_router, sw_we_gate, sw_we_up, sw_we_down):
    b, s, d = x.shape
    scale = HEAD_DIM ** -0.5
    lane_head = jnp.arange(MEM_WIDTH) // HEAD_DIM
    head_ones = (lane_head[:, None] == lane_head[None, :]).astype(BF16)
    tile4 = lambda gvec: jnp.tile(gvec, MEM_HEADS).reshape(1, MEM_WIDTH)

    kexp, vexp = _mem_kv(mem, mem_norm_g, w_mem_kv.astype(BF16), tile4(mem_k_norm_g), head_ones)

    h = x
    depth = cv_w_in.shape[0] + sw_w_in.shape[0]
    for i in range(depth):
        j = i // 2
        if i % 2 == 0:
            h = _conv_mixer(h, cv_attn_norm_g[j], cv_w_in[j].astype(BF16), cv_b_glu[j], cv_dw_w[j], cv_dw_b[j],
                            cv_ln_g[j], cv_ln_b[j], tile4(cv_memq_norm_g[j]) * scale, head_ones, kexp, vexp,
                            cv_w_out[j].astype(BF16))
            h = _dense_ffn(h.reshape(b * s, d), cv_ffn_norm_g[j], cv_w_gate[j].astype(BF16),
                           cv_w_up[j].astype(BF16), cv_w_down[j].astype(BF16)).reshape(b, s, d)
        else:
            qw = SWA_KV_HEADS * SWA_GROUP * HEAD_DIM
            w_in = jnp.concatenate([_regroup_heads(sw_w_in[j][:, :qw], 1), sw_w_in[j][:, qw:]], axis=1)
            w_out = jnp.concatenate([_regroup_heads(sw_w_out[j][:qw], 0), sw_w_out[j][qw:]], axis=0)
            h = _swa_mixer(h, sw_attn_norm_g[j], w_in.astype(BF16), tile4(sw_q_norm_g[j]) * scale,
                           tile4(sw_k_norm_g[j]), sw_sinks[j], tile4(sw_memq_norm_g[j]) * scale, head_ones,
                           kexp, vexp, w_out.astype(BF16))
            h = _moe(h.reshape(b * s, d), sw_ffn_norm_g[j], sw_router[j], sw_we_gate[j], sw_we_up[j],
                     sw_we_down[j]).reshape(b, s, d)
    return h
```
